```python
import jax, jax.numpy as jnp
from jax import lax
import numpy as np

D_MODEL = 1024
BATCH = 16
SEQ = 2048
DEPTH = 2

N_A_LAYERS = DEPTH // 2
N_B_LAYERS = DEPTH - N_A_LAYERS

A_HEADS = 16
A_HEAD_DIM = D_MODEL // A_HEADS
DILATED_PATTERNS = ((128, 1), (512, 4), (2048, 16))
BAND = max(w // d for w, d in DILATED_PATTERNS)

B_HEADS = 16
B_NOPE_DIM = 64
B_ROPE_DIM = 32
B_V_DIM = 64
Q_RANK = 384
KV_RANK = 256
ROPE_BASE = 10000.0
Q_BLOCK = 128

D_FF = 2816
CONV_WIDTH = 3

ALPHA = (2.0 * DEPTH) ** 0.25
BETA = (8.0 * DEPTH) ** -0.25
LN_EPS = 1e-5
RMS_EPS = 1e-6

kernel_name = 'yoco_dilated_mla_convffn_deepnorm'


def layer_norm(x, g, b):
    xf = x.astype(jnp.float32)
    mu = jnp.mean(xf, axis=-1, keepdims=True)
    var = jnp.mean(jnp.square(xf - mu), axis=-1, keepdims=True)
    return ((xf - mu) * lax.rsqrt(var + LN_EPS) * g.astype(jnp.float32) + b.astype(jnp.float32)).astype(x.dtype)


def rms_norm(x, g):
    xf = x.astype(jnp.float32)
    ms = jnp.mean(jnp.square(xf), axis=-1, keepdims=True)
    return (xf * lax.rsqrt(ms + RMS_EPS) * g.astype(jnp.float32)).astype(x.dtype)


def alibi_slopes(n_heads):
    return jnp.asarray([2.0 ** (-8.0 * (h + 1) / n_heads) for h in range(n_heads)], dtype=jnp.float32)


def rope_tables(seq):
    inv_freq = ROPE_BASE ** (-jnp.arange(0, B_ROPE_DIM, 2, dtype=jnp.float32) / B_ROPE_DIM)
    ang = jnp.arange(seq, dtype=jnp.float32)[:, None] * inv_freq[None, :]
    return jnp.cos(ang), jnp.sin(ang)


def apply_rope(t, cos, sin):
    tf = t.astype(jnp.float32)
    t1, t2 = jnp.split(tf, 2, axis=-1)
    return jnp.concatenate([t1 * cos - t2 * sin, t2 * cos + t1 * sin], axis=-1).astype(t.dtype)


def dilated_branch(q, k, v, window, dilation, slopes):
    bsz, seq, heads, dh = q.shape
    sub_len = seq // dilation
    n_blk = -(-sub_len // BAND)
    sub_pad = n_blk * BAND
    steps = window // dilation

    def to_sub(t):
        t = t.reshape(bsz, sub_len, dilation, heads, dh)
        return jnp.pad(t, ((0, 0), (0, sub_pad - sub_len), (0, 0), (0, 0), (0, 0)))

    def band(t):
        t = jnp.pad(t, ((0, 0), (BAND, 0), (0, 0), (0, 0), (0, 0)))
        t = t.reshape(bsz, n_blk + 1, BAND, dilation, heads, dh)
        return jnp.concatenate([t[:, :-1], t[:, 1:]], axis=2)

    qb = to_sub(q).reshape(bsz, n_blk, BAND, dilation, heads, dh)
    kb = band(to_sub(k))
    vb = band(to_sub(v))

    s = jnp.einsum('bnqrhd,bnkrhd->bnrhqk', qb, kb).astype(jnp.float32) * (dh ** -0.5)
    qi = jnp.arange(BAND)[:, None]
    ki = jnp.arange(2 * BAND)[None, :]
    back = BAND + qi - ki
    kpos = jnp.arange(n_blk)[:, None, None] * BAND - BAND + ki[None]
    valid = (back >= 0) & (back <= steps) & (kpos >= 0)
    bias = -slopes[:, None, None] * (dilation * back).astype(jnp.float32)[None]
    s = jnp.where(valid[:, None, None], s + bias, -jnp.inf)
    m = jnp.max(s, axis=-1, keepdims=True)
    p = jnp.exp(s - m)
    den = jnp.sum(p, axis=-1)
    lse = m[..., 0] + jnp.log(den)
    o = jnp.einsum('bnrhqk,bnkrhd->bnqrhd', p.astype(vb.dtype), vb).astype(jnp.float32)
    den_t = jnp.transpose(den, (0, 1, 4, 2, 3))
    o = o / den_t[..., None]
    o = o.reshape(bsz, sub_pad, dilation, heads, dh)[:, :sub_len].reshape(bsz, seq, heads, dh)
    lse = jnp.transpose(lse, (0, 1, 4, 2, 3)).reshape(bsz, sub_pad, dilation, heads)[:, :sub_len]
    return o, lse.reshape(bsz, seq, heads)


def dilated_mixer(x, w_qkv, w_o):
    bsz, seq, _ = x.shape
    qkv = (x @ w_qkv).reshape(bsz, seq, 3, A_HEADS, A_HEAD_DIM)
    q, k, v = qkv[:, :, 0], qkv[:, :, 1], qkv[:, :, 2]
    slopes = alibi_slopes(A_HEADS)
    outs, lses = [], []
    for window, dilation in DILATED_PATTERNS:
        o, l = dilated_branch(q, k, v, window, dilation, slopes)
        outs.append(o)
        lses.append(l)
    wts = jax.nn.softmax(jnp.stack(lses, axis=0), axis=0)
    o = jnp.einsum('pbsh,pbshd->bshd', wts, jnp.stack(outs, axis=0))
    return o.astype(x.dtype).reshape(bsz, seq, A_HEADS * A_HEAD_DIM) @ w_o


def shared_latent_kv(x, w_dkv, kv_norm_g, w_kr, w_uk, w_uv, cos, sin):
    bsz, seq, _ = x.shape
    c_kv = rms_norm(x @ w_dkv, kv_norm_g)
    k_nope = (c_kv @ w_uk).reshape(bsz, seq, B_HEADS, B_NOPE_DIM)
    v = (c_kv @ w_uv).reshape(bsz, seq, B_HEADS, B_V_DIM)
    k_rope = apply_rope(x @ w_kr, cos[None], sin[None])
    return k_nope, k_rope, v


def mla_mixer(x, w_dq, q_norm_g, w_uq, w_o, k_nope, k_rope, v, cos, sin):
    bsz, seq, _ = x.shape
    q = (rms_norm(x @ w_dq, q_norm_g) @ w_uq).reshape(bsz, seq, B_HEADS, B_NOPE_DIM + B_ROPE_DIM)
    q_nope = q[..., :B_NOPE_DIM]
    q_rope = apply_rope(q[..., B_NOPE_DIM:], cos[None, :, None], sin[None, :, None])
    n_q = seq // Q_BLOCK
    qn_blocks = q_nope.reshape(bsz, n_q, Q_BLOCK, B_HEADS, B_NOPE_DIM).transpose(1, 0, 2, 3, 4)
    qr_blocks = q_rope.reshape(bsz, n_q, Q_BLOCK, B_HEADS, B_ROPE_DIM).transpose(1, 0, 2, 3, 4)
    kpos = jnp.arange(seq)
    scale = (B_NOPE_DIM + B_ROPE_DIM) ** -0.5

    def attend(args):
        qn, qr, blk = args
        s = (jnp.einsum('bqhd,bkhd->bhqk', qn, k_nope)
             + jnp.einsum('bqhr,bkr->bhqk', qr, k_rope)).astype(jnp.float32) * scale
        qpos = blk * Q_BLOCK + jnp.arange(Q_BLOCK)
        s = jnp.where(kpos[None, :] <= qpos[:, None], s, -jnp.inf)
        p = jax.nn.softmax(s, axis=-1)
        return jnp.einsum('bhqk,bkhd->bqhd', p.astype(v.dtype), v)

    o = lax.map(attend, (qn_blocks, qr_blocks, jnp.arange(n_q)))
    o = o.transpose(1, 0, 2, 3, 4).reshape(bsz, seq, B_HEADS * B_V_DIM)
    return o @ w_o


def conv_ffn(x, w_in, conv_w, conv_b, w_out):
    seq = x.shape[1]
    u = x @ w_in
    up = jnp.pad(u, ((0, 0), (CONV_WIDTH - 1, 0), (0, 0)))
    c = conv_b
    for j in range(CONV_WIDTH):
        c = c + conv_w[j] * up[:, j:j + seq]
    gate, val = jnp.split(c, 2, axis=-1)
    return (jax.nn.silu(gate) * val) @ w_out


def setup_inputs(seed: int = 0) -> dict:
    key = jax.random.key(seed)
    ks = iter(jax.random.split(key, 32))

    def dense(shape, fan_in, scale=1.0):
        return jax.random.normal(next(ks), shape, jnp.float32) * (scale * fan_in ** -0.5)

    def gain(shape):
        return 1.0 + 0.02 * jax.random.normal(next(ks), shape, jnp.float32)

    def bias(shape):
        return 0.02 * jax.random.normal(next(ks), shape, jnp.float32)

    hd_a = A_HEADS * A_HEAD_DIM
    v_scale = jnp.concatenate([jnp.ones((2 * hd_a,), jnp.float32), jnp.full((hd_a,), BETA, jnp.float32)])
    return {
        'x': jax.random.normal(next(ks), (BATCH, SEQ, D_MODEL), jnp.float32),
        'a_w_qkv': dense((N_A_LAYERS, D_MODEL, 3 * hd_a), D_MODEL) * v_scale,
        'a_w_o': dense((N_A_LAYERS, hd_a, D_MODEL), hd_a, BETA),
        'kv_w_dkv': dense((D_MODEL, KV_RANK), D_MODEL),
        'kv_norm_g': gain((KV_RANK,)),
        'kv_w_kr': dense((D_MODEL, B_ROPE_DIM), D_MODEL),
        'kv_w_uk': dense((KV_RANK, B_HEADS * B_NOPE_DIM), KV_RANK),
        'kv_w_uv': dense((KV_RANK, B_HEADS * B_V_DIM), KV_RANK, BETA),
        'b_w_dq': dense((N_B_LAYERS, D_MODEL, Q_RANK), D_MODEL),
        'b_q_norm_g': gain((N_B_LAYERS, Q_RANK)),
        'b_w_uq': dense((N_B_LAYERS, Q_RANK, B_HEADS * (B_NOPE_DIM + B_ROPE_DIM)), Q_RANK),
        'b_w_o': dense((N_B_LAYERS, B_HEADS * B_V_DIM, D_MODEL), B_HEADS * B_V_DIM, BETA),
        'ffn_w_in': dense((DEPTH, D_MODEL, 2 * D_FF), D_MODEL),
        'ffn_conv_w': dense((DEPTH, CONV_WIDTH, 2 * D_FF), CONV_WIDTH),
        'ffn_conv_b': bias((DEPTH, 2 * D_FF)),
        'ffn_w_out': dense((DEPTH, D_FF, D_MODEL), D_FF, BETA),
        'ln_mix_g': gain((DEPTH, D_MODEL)),
        'ln_mix_b': bias((DEPTH, D_MODEL)),
        'ln_ffn_g': gain((DEPTH, D_MODEL)),
        'ln_ffn_b': bias((DEPTH, D_MODEL)),
    }


def reference(x, a_w_qkv, a_w_o, kv_w_dkv, kv_norm_g, kv_w_kr, kv_w_uk, kv_w_uv,
              b_w_dq, b_q_norm_g, b_w_uq, b_w_o, ffn_w_in, ffn_conv_w, ffn_conv_b, ffn_w_out,
              ln_mix_g, ln_mix_b, ln_ffn_g, ln_ffn_b):
    cos, sin = rope_tables(x.shape[1])
    k_nope = k_rope = v_shared = None
    for layer in range(DEPTH):
        if layer == N_A_LAYERS:
            k_nope, k_rope, v_shared = shared_latent_kv(x, kv_w_dkv, kv_norm_g, kv_w_kr,
                                                        kv_w_uk, kv_w_uv, cos, sin)
        if layer < N_A_LAYERS:
            mix = dilated_mixer(x, a_w_qkv[layer], a_w_o[layer])
        else:
            j = layer - N_A_LAYERS
            mix = mla_mixer(x, b_w_dq[j], b_q_norm_g[j], b_w_uq[j], b_w_o[j],
                            k_nope, k_rope, v_shared, cos, sin)
        x = layer_norm(ALPHA * x + mix, ln_mix_g[layer], ln_mix_b[layer])
        f = conv_ffn(x, ffn_w_in[layer], ffn_conv_w[layer], ffn_conv_b[layer], ffn_w_out[layer])
        x = layer_norm(ALPHA * x + f, ln_ffn_g[layer], ln_ffn_b[layer])
    return x
```

```python
import functools

import jax
import jax.numpy as jnp
from jax import lax
from jax.experimental import pallas as pl
from jax.experimental.pallas import tpu as pltpu

D_MODEL = 1024
DEPTH = 2
A_HEADS = 16
A_HEAD_DIM = 64
DILATED_PATTERNS = ((128, 1), (512, 4), (2048, 16))
BAND = 128
B_HEADS = 16
B_NOPE_DIM = 64
B_ROPE_DIM = 32
B_V_DIM = 64
Q_RANK = 384
KV_RANK = 256
ROPE_BASE = 10000.0
D_FF = 2816
CONV_WIDTH = 3
ALPHA = (2.0 * DEPTH) ** 0.25
LN_EPS = 1e-5
RMS_EPS = 1e-6

LANES = 128
SUBLANES = 8
NEG = -1e30
VMEM_LIMIT = 56 * 1024 * 1024

F32 = jnp.float32
BF16 = jnp.bfloat16


def _const_spec(shape):
    nd = len(shape)
    return pl.BlockSpec(shape, lambda *_: (0,) * nd, pipeline_mode=pl.Buffered(1))


def _params(*sem):
    return pltpu.CompilerParams(dimension_semantics=sem, vmem_limit_bytes=VMEM_LIMIT)


def _layer_norm(y, g, b):
    mu = jnp.mean(y, axis=-1, keepdims=True)
    yc = y - mu
    var = jnp.mean(yc * yc, axis=-1, keepdims=True)
    return yc * lax.rsqrt(var + LN_EPS) * g + b


def _qkv_kernel(x_ref, w_ref, o_ref):
    o_ref[...] = jnp.dot(x_ref[...].astype(BF16), w_ref[...], preferred_element_type=F32)


def _qkv_proj(x2d, w):
    t, d = x2d.shape
    n = w.shape[1]
    tm = 512
    return pl.pallas_call(
        _qkv_kernel,
        grid=(t // tm,),
        in_specs=[pl.BlockSpec((tm, d), lambda i: (i, 0)), _const_spec((d, n))],
        out_specs=pl.BlockSpec((tm, n), lambda i: (i, 0)),
        out_shape=jax.ShapeDtypeStruct((t, n), F32),
        compiler_params=_params("parallel"),
        name="qkv_proj",
    )(x2d, w)


def _dilated_kernel(slopes_ref, q_ref, k_ref, v_ref, o_ref, ob_ref, lse_ref, *, seq):
    hp = pl.program_id(1)
    lane = lax.broadcasted_iota(jnp.int32, (1, LANES), 1)
    head0 = lane < A_HEAD_DIM
    q_scale = A_HEAD_DIM ** -0.5

    def unit(branch, dilation, q_start, k_start, width):
        ql = lax.broadcasted_iota(jnp.int32, (BAND, width), 0)
        kl = lax.broadcasted_iota(jnp.int32, (BAND, width), 1)
        back = ql - kl + (width - BAND)
        valid = (back >= 0) & (back <= BAND)
        dist = (back * dilation).astype(F32)
        q = (q_ref[0, pl.ds(q_start, BAND, stride=dilation), :] * q_scale).astype(BF16)
        k = k_ref[0, pl.ds(k_start, width, stride=dilation), :].astype(BF16)
        v = v_ref[0, pl.ds(k_start, width, stride=dilation), :].astype(BF16)
        outs, lses = [], []
        for hh in range(2):
            mask = head0 if hh == 0 else jnp.logical_not(head0)
            qm = jnp.where(mask, q, jnp.zeros_like(q))
            s = lax.dot_general(qm, k, (((1,), (1,)), ((), ())), preferred_element_type=F32)
            s = jnp.where(valid, s - slopes_ref[2 * hp + hh] * dist, NEG)
            m = jnp.max(s, axis=-1, keepdims=True)
            p = jnp.exp(s - m)
            den = jnp.sum(p, axis=-1, keepdims=True)
            o = jnp.dot(p.astype(BF16), v, preferred_element_type=F32)
            outs.append(o / den)
            lses.append(m + jnp.log(den))
        rows = pl.ds(q_start, BAND, stride=dilation)
        ob_ref[branch, rows, :] = jnp.where(head0, outs[0], outs[1])
        lse_ref[branch, rows, :] = jnp.where(head0, lses[0], lses[1])

    for branch, (_, dilation) in enumerate(DILATED_PATTERNS):
        n_blk = seq // dilation // BAND
        blk_stride = BAND * dilation

        def residue(r, carry, branch=branch, dilation=dilation, n_blk=n_blk, blk_stride=blk_stride):
            unit(branch, dilation, r, r, BAND)

            def later(n, c):
                unit(branch, dilation, n * blk_stride + r, (n - 1) * blk_stride + r, 2 * BAND)
                return c

            if n_blk > 1:
                lax.fori_loop(1, n_blk, later, 0)
            return carry

        lax.fori_loop(0, dilation, residue, 0)

    rows_per = 256

    def merge(i, carry):
        rows = pl.ds(pl.multiple_of(i * rows_per, rows_per), rows_per)
        l0, l1, l2 = lse_ref[0, rows, :], lse_ref[1, rows, :], lse_ref[2, rows, :]
        mx = jnp.maximum(jnp.maximum(l0, l1), l2)
        w0, w1, w2 = jnp.exp(l0 - mx), jnp.exp(l1 - mx), jnp.exp(l2 - mx)
        num = w0 * ob_ref[0, rows, :] + w1 * ob_ref[1, rows, :] + w2 * ob_ref[2, rows, :]
        o_ref[0, rows, :] = (num / (w0 + w1 + w2)).astype(o_ref.dtype)
        return carry

    lax.fori_loop(0, seq // rows_per, merge, 0)


def _dilated_attn(qkv, slopes):
    bsz, seq, _ = qkv.shape
    n_pairs = A_HEADS * A_HEAD_DIM // LANES
    assert all(w // d == BAND and seq % (d * BAND) == 0 for w, d in DILATED_PATTERNS)

    def col_spec(part):
        return pl.BlockSpec((1, seq, LANES), lambda b, h, part=part: (b, 0, part * n_pairs + h))

    return pl.pallas_call(
        functools.partial(_dilated_kernel, seq=seq),
        grid=(bsz, n_pairs),
        in_specs=[pl.BlockSpec(memory_space=pltpu.SMEM), col_spec(0), col_spec(1), col_spec(2)],
        out_specs=pl.BlockSpec((1, seq, LANES), lambda b, h: (b, 0, h)),
        out_shape=jax.ShapeDtypeStruct((bsz, seq, A_HEADS * A_HEAD_DIM), BF16),
        scratch_shapes=[pltpu.VMEM((3, seq, LANES), F32), pltpu.VMEM((3, seq, LANES), F32)],
        compiler_params=_params("parallel", "parallel"),
        name="dilated_attn",
    )(slopes, qkv, qkv, qkv)


def _proj_ln_kernel(o_ref, x_ref, w_ref, g_ref, b_ref, y_ref):
    mix = jnp.dot(o_ref[...], w_ref[...], preferred_element_type=F32)
    y_ref[...] = _layer_norm(ALPHA * x_ref[...] + mix, g_ref[...], b_ref[...])


def _proj_ln(o2d, x2d, w, g, b):
    t, d = x2d.shape
    k = o2d.shape[1]
    tm = 512
    return pl.pallas_call(
        _proj_ln_kernel,
        grid=(t // tm,),
        in_specs=[pl.BlockSpec((tm, k), lambda i: (i, 0)), pl.BlockSpec((tm, d), lambda i: (i, 0)),
                  _const_spec((k, d)), _const_spec((1, d)), _const_spec((1, d))],
        out_specs=pl.BlockSpec((tm, d), lambda i: (i, 0)),
        out_shape=jax.ShapeDtypeStruct((t, d), F32),
        compiler_params=_params("parallel"),
        name="proj_ln",
    )(o2d, x2d, w, g, b)


FFN_COLS = 256


def _ffn_kernel(x_ref, win_ref, cw_ref, cb_ref, wout_ref, g_ref, b_ref, y_ref, u_ref, h_ref,
                *, tm, tiles_per_seq):
    i = pl.program_id(0)
    halo = SUBLANES

    @pl.when(i % tiles_per_seq == 0)
    def _():
        u_ref[0:halo, :] = jnp.zeros((halo, 2 * D_FF), F32)

    x = x_ref[...]
    u_ref[halo:halo + tm, :] = jnp.dot(x.astype(BF16), win_ref[...], preferred_element_type=F32)

    for c in range(D_FF // FFN_COLS):
        def conv(col):
            cols = slice(col, col + FFN_COLS)
            acc = cb_ref[:, cols] + cw_ref[2:3, cols] * u_ref[halo:halo + tm, cols]
            acc = acc + cw_ref[1:2, cols] * u_ref[halo - 1:halo - 1 + tm, cols]
            return acc + cw_ref[0:1, cols] * u_ref[halo - 2:halo - 2 + tm, cols]

        gate = conv(c * FFN_COLS)
        val = conv(D_FF + c * FFN_COLS)
        h_ref[:, c * FFN_COLS:(c + 1) * FFN_COLS] = (jax.nn.silu(gate) * val).astype(BF16)

    u_ref[0:halo, :] = u_ref[tm:tm + halo, :]

    f = jnp.dot(h_ref[...], wout_ref[...], preferred_element_type=F32)
    y_ref[...] = _layer_norm(ALPHA * x + f, g_ref[...], b_ref[...])


def _conv_ffn_ln(x2d, seq, w_in, conv_w, conv_b, w_out, g, b):
    t, d = x2d.shape
    tm = 256
    assert seq % tm == 0 and D_FF % FFN_COLS == 0 and CONV_WIDTH - 1 <= SUBLANES
    return pl.pallas_call(
        functools.partial(_ffn_kernel, tm=tm, tiles_per_seq=seq // tm),
        grid=(t // tm,),
        in_specs=[pl.BlockSpec((tm, d), lambda i: (i, 0)), _const_spec((d, 2 * D_FF)),
                  _const_spec((CONV_WIDTH, 2 * D_FF)), _const_spec((1, 2 * D_FF)),
                  _const_spec((D_FF, d)), _const_spec((1, d)), _const_spec((1, d))],
        out_specs=pl.BlockSpec((tm, d), lambda i: (i, 0)),
        out_shape=jax.ShapeDtypeStruct((t, d), F32),
        scratch_shapes=[pltpu.VMEM((tm + SUBLANES, 2 * D_FF), F32), pltpu.VMEM((tm, D_FF), BF16)],
        compiler_params=_params("arbitrary"),
        name="conv_ffn_ln",
    )(x2d, w_in, conv_w, conv_b, w_out, g, b)


def _rope(t, cos, sin_signed):
    lane = lax.broadcasted_iota(jnp.int32, t.shape, 1)
    half = B_ROPE_DIM // 2
    first = (lane % B_ROPE_DIM) < half
    swapped = jnp.where(first, pltpu.roll(t, LANES - half, 1), pltpu.roll(t, half, 1))
    return t * cos + swapped * sin_signed


def _rms(c, g):
    return c * lax.rsqrt(jnp.mean(c * c, axis=-1, keepdims=True) + RMS_EPS) * g


def _mla_proj_kernel(x_ref, wd_ref, kvg_ref, qg_ref, wuk_ref, wuv_ref, wuqn_ref, wuqr_ref,
                     cos_ref, sin_ref, qn_ref, qr_ref, kn_ref, kr_ref, v_ref):
    x = x_ref[...].astype(BF16)
    down = jnp.dot(x, wd_ref[...], preferred_element_type=F32)
    cos, sin = cos_ref[...], sin_ref[...]

    c_kv = _rms(down[:, :KV_RANK], kvg_ref[...]).astype(BF16)
    kn_ref[...] = jnp.dot(c_kv, wuk_ref[...], preferred_element_type=F32).astype(BF16)
    v_ref[...] = jnp.dot(c_kv, wuv_ref[...], preferred_element_type=F32).astype(BF16)
    kr_ref[...] = _rope(down[:, KV_RANK:KV_RANK + LANES], cos, sin).astype(BF16)

    scale = (B_NOPE_DIM + B_ROPE_DIM) ** -0.5
    c_q = _rms(down[:, KV_RANK + LANES:], qg_ref[...]).astype(BF16)
    qn_ref[...] = (jnp.dot(c_q, wuqn_ref[...], preferred_element_type=F32) * scale).astype(BF16)
    q_rope = jnp.dot(c_q, wuqr_ref[...], preferred_element_type=F32)
    for c in range(q_rope.shape[1] // LANES):
        cols = slice(c * LANES, (c + 1) * LANES)
        qr_ref[:, cols] = (_rope(q_rope[:, cols], cos, sin) * scale).astype(BF16)


def _mla_proj(x2d, seq, w_down, kv_g, q_g, w_uk, w_uv, w_uqn, w_uqr, cos_t, sin_t):
    t, d = x2d.shape
    tm = 512
    n_seq_tiles = seq // tm
    hn, hr, hv = B_HEADS * B_NOPE_DIM, B_HEADS * B_ROPE_DIM, B_HEADS * B_V_DIM
    row = lambda n: pl.BlockSpec((tm, n), lambda i: (i, 0))
    tab = pl.BlockSpec((tm, LANES), lambda i: (i % n_seq_tiles, 0))
    return pl.pallas_call(
        _mla_proj_kernel,
        grid=(t // tm,),
        in_specs=[row(d), _const_spec(w_down.shape), _const_spec((1, KV_RANK)), _const_spec((1, Q_RANK)),
                  _const_spec(w_uk.shape), _const_spec(w_uv.shape), _const_spec(w_uqn.shape),
                  _const_spec(w_uqr.shape), tab, tab],
        out_specs=[row(hn), row(hr), row(hn), row(LANES), row(hv)],
        out_shape=[jax.ShapeDtypeStruct((t, hn), BF16), jax.ShapeDtypeStruct((t, hr), BF16),
                   jax.ShapeDtypeStruct((t, hn), BF16), jax.ShapeDtypeStruct((t, LANES), BF16),
                   jax.ShapeDtypeStruct((t, hv), BF16)],
        compiler_params=_params("parallel"),
        name="mla_proj",
    )(x2d, w_down, kv_g, q_g, w_uk, w_uv, w_uqn, w_uqr, cos_t, sin_t)


MLA_GROUP = 4
MLA_TQ = 256
MLA_TK = 256


def _mla_attn_kernel(qn_ref, qr_ref, kn_ref, kr_ref, v_ref, o_ref):
    qi = pl.program_id(2)
    tq, tk = MLA_TQ, MLA_TK
    lane = lax.broadcasted_iota(jnp.int32, (1, LANES), 1)
    qr_all = qr_ref[0]
    nt = (((1,), (1,)), ((), ()))

    pair_out = []
    for pair in range(MLA_GROUP // 2):
        cols = slice(pair * LANES, (pair + 1) * LANES)
        qn_pair = qn_ref[0, :, cols]
        head_out = []
        for hh in range(2):
            h = pair * 2 + hh
            qn = jnp.where(lane // B_NOPE_DIM == hh, qn_pair, jnp.zeros_like(qn_pair))
            qr = jnp.where(lane // B_ROPE_DIM == h, qr_all, jnp.zeros_like(qr_all))
            qcat = jnp.concatenate([qn, qr], axis=1)

            def scores(ks, qcat=qcat, cols=cols):
                kcat = jnp.concatenate([kn_ref[0, pl.ds(ks, tk), cols], kr_ref[0, pl.ds(ks, tk), :]], axis=1)
                return lax.dot_general(qcat, kcat, nt, preferred_element_type=F32)

            def update(carry, s, ks, cols=cols):
                m, l, acc = carry
                m_new = jnp.maximum(m, jnp.max(s, axis=-1, keepdims=True))
                a = jnp.exp(m - m_new)
                p = jnp.exp(s - m_new)
                l = a * l + jnp.sum(p, axis=-1, keepdims=True)
                pv = jnp.dot(p.astype(BF16), v_ref[0, pl.ds(ks, tk), cols], preferred_element_type=F32)
                return m_new, l, a * acc + pv

            def body(j, carry):
                ks = pl.multiple_of(j * tk, tk)
                return update(carry, scores(ks), ks)

            init = (jnp.full((tq, 1), NEG, F32), jnp.zeros((tq, 1), F32), jnp.zeros((tq, LANES), F32))
            carry = lax.fori_loop(0, qi, body, init)
            ks = pl.multiple_of(qi * tk, tk)
            ql = lax.broadcasted_iota(jnp.int32, (tq, tk), 0)
            kl = lax.broadcasted_iota(jnp.int32, (tq, tk), 1)
            s = jnp.where(kl <= ql, scores(ks), NEG)
            _, l, acc = update(carry, s, ks)
            head_out.append(acc / l)
        pair_out.append(jnp.where(lane < B_V_DIM, head_out[0], head_out[1]))
    o_ref[0] = jnp.concatenate(pair_out, axis=1).astype(o_ref.dtype)


def _mla_attn(qn, qr, kn, kr, v):
    bsz, seq, _ = qn.shape
    n_groups = B_HEADS // MLA_GROUP
    gn, gr = MLA_GROUP * B_NOPE_DIM, MLA_GROUP * B_ROPE_DIM
    assert MLA_TQ == MLA_TK and seq % MLA_TQ == 0 and gr == LANES and B_NOPE_DIM == B_V_DIM
    return pl.pallas_call(
        _mla_attn_kernel,
        grid=(bsz, n_groups, seq // MLA_TQ),
        in_specs=[pl.BlockSpec((1, MLA_TQ, gn), lambda b, g, i: (b, i, g)),
                  pl.BlockSpec((1, MLA_TQ, gr), lambda b, g, i: (b, i, g)),
                  pl.BlockSpec((1, seq, gn), lambda b, g, i: (b, 0, g)),
                  pl.BlockSpec((1, seq, LANES), lambda b, g, i: (b, 0, 0)),
                  pl.BlockSpec((1, seq, gn), lambda b, g, i: (b, 0, g))],
        out_specs=pl.BlockSpec((1, MLA_TQ, gn), lambda b, g, i: (b, i, g)),
        out_shape=jax.ShapeDtypeStruct((bsz, seq, B_HEADS * B_V_DIM), BF16),
        compiler_params=_params("parallel", "parallel", "arbitrary"),
        name="mla_attn",
    )(qn, qr, kn, kr, v)


def _rope_tables(seq):
    inv_freq = ROPE_BASE ** (-jnp.arange(0, B_ROPE_DIM, 2, dtype=F32) / B_ROPE_DIM)
    ang = jnp.arange(seq, dtype=F32)[:, None] * inv_freq[None, :]
    cos, sin = jnp.cos(ang), jnp.sin(ang)
    reps = LANES // B_ROPE_DIM
    cos_t = jnp.tile(jnp.concatenate([cos, cos], axis=1), (1, reps))
    sin_t = jnp.tile(jnp.concatenate([-sin, sin], axis=1), (1, reps))
    return cos_t, sin_t


def kernel(x, a_w_qkv, a_w_o, kv_w_dkv, kv_norm_g, kv_w_kr, kv_w_uk, kv_w_uv, b_w_dq, b_q_norm_g, b_w_uq, b_w_o, ffn_w_in, ffn_conv_w, ffn_conv_b, ffn_w_out, ln_mix_g, ln_mix_b, ln_ffn_g, ln_ffn_b):
    bsz, seq, d = x.shape
    t = bsz * seq
    x2d = x.reshape(t, d)
    row = lambda a: a.reshape(1, -1)

    def ffn(xin, layer):
        return _conv_ffn_ln(xin, seq, ffn_w_in[layer].astype(BF16), ffn_conv_w[layer], row(ffn_conv_b[layer]),
                            ffn_w_out[layer].astype(BF16), row(ln_ffn_g[layer]), row(ln_ffn_b[layer]))

    slopes = jnp.asarray([2.0 ** (-8.0 * (h + 1) / A_HEADS) for h in range(A_HEADS)], dtype=F32)
    qkv = _qkv_proj(x2d, a_w_qkv[0].astype(BF16))
    o = _dilated_attn(qkv.reshape(bsz, seq, -1), slopes)
    x2d = _proj_ln(o.reshape(t, -1), x2d, a_w_o[0].astype(BF16), row(ln_mix_g[0]), row(ln_mix_b[0]))
    x2d = ffn(x2d, 0)

    cos_t, sin_t = _rope_tables(seq)
    w_down = jnp.concatenate([kv_w_dkv, jnp.tile(kv_w_kr, (1, LANES // B_ROPE_DIM)), b_w_dq[0]], axis=1)
    w_uq = b_w_uq[0].reshape(Q_RANK, B_HEADS, B_NOPE_DIM + B_ROPE_DIM)
    w_uqn = w_uq[:, :, :B_NOPE_DIM].reshape(Q_RANK, -1)
    w_uqr = w_uq[:, :, B_NOPE_DIM:].reshape(Q_RANK, -1)
    qn, qr, kn, kr, v = _mla_proj(x2d, seq, w_down.astype(BF16), row(kv_norm_g), row(b_q_norm_g[0]),
                                  kv_w_uk.astype(BF16), kv_w_uv.astype(BF16), w_uqn.astype(BF16),
                                  w_uqr.astype(BF16), cos_t, sin_t)
    shp = lambda a: a.reshape(bsz, seq, -1)
    o = _mla_attn(shp(qn), shp(qr), shp(kn), shp(kr), shp(v))
    x2d = _proj_ln(o.reshape(t, -1), x2d, b_w_o[0].astype(BF16), row(ln_mix_g[1]), row(ln_mix_b[1]))
    x2d = ffn(x2d, 1)
    return x2d.reshape(bsz, seq, d)
```

```python
import functools

import jax
import jax.numpy as jnp
from jax import lax
from jax.experimental import pallas as pl
from jax.experimental.pallas import tpu as pltpu

D_MODEL = 1024
DEPTH = 2
A_HEADS = 16
A_HEAD_DIM = 64
DILATED_PATTERNS = ((128, 1), (512, 4), (2048, 16))
BAND = 128
B_HEADS = 16
B_NOPE_DIM = 64
B_ROPE_DIM = 32
B_V_DIM = 64
Q_RANK = 384
KV_RANK = 256
ROPE_BASE = 10000.0
D_FF = 2816
CONV_WIDTH = 3
ALPHA = (2.0 * DEPTH) ** 0.25
LN_EPS = 1e-5
RMS_EPS = 1e-6

LANES = 128
SUBLANES = 8
NEG = -1e30
VMEM_LIMIT = 56 * 1024 * 1024

F32 = jnp.float32
BF16 = jnp.bfloat16


def _const_spec(shape):
    nd = len(shape)
    return pl.BlockSpec(shape, lambda *_: (0,) * nd, pipeline_mode=pl.Buffered(1))


def _params(*sem):
    return pltpu.CompilerParams(dimension_semantics=sem, vmem_limit_bytes=VMEM_LIMIT)


def _layer_norm(y, g, b):
    mu = jnp.mean(y, axis=-1, keepdims=True)
    yc = y - mu
    var = jnp.mean(yc * yc, axis=-1, keepdims=True)
    return yc * lax.rsqrt(var + LN_EPS) * g + b


def _qkv_kernel(x_ref, w_ref, o_ref):
    o_ref[...] = jnp.dot(x_ref[...].astype(BF16), w_ref[...], preferred_element_type=F32)


def _qkv_proj(x2d, w):
    t, d = x2d.shape
    n = w.shape[1]
    tm = 512
    return pl.pallas_call(
        _qkv_kernel,
        grid=(t // tm,),
        in_specs=[pl.BlockSpec((tm, d), lambda i: (i, 0)), _const_spec((d, n))],
        out_specs=pl.BlockSpec((tm, n), lambda i: (i, 0)),
        out_shape=jax.ShapeDtypeStruct((t, n), F32),
        compiler_params=_params("parallel"),
        name="qkv_proj",
    )(x2d, w)


DIL_GROUP_FIRST = 8
DIL_GROUP_RESIDUES = 2
DIL_GROUP_LATER = 4


def _dilated_kernel(slopes_ref, q_ref, k_ref, v_ref, o_ref, ob_ref, lse_ref, bias_ref, *, seq):
    hp = pl.program_id(1)
    lane = lax.broadcasted_iota(jnp.int32, (1, LANES), 1)
    head0 = lane < A_HEAD_DIM
    head_masks = (head0, jnp.logical_not(head0))
    q_scale = A_HEAD_DIM ** -0.5
    ones = jnp.ones((1, LANES), BF16)

    def fill_bias(dilation):
        for off, width in ((0, BAND), (BAND, 2 * BAND)):
            ql = lax.broadcasted_iota(jnp.int32, (BAND, width), 0)
            kl = lax.broadcasted_iota(jnp.int32, (BAND, width), 1)
            back = ql - kl + (width - BAND)
            valid = (back >= 0) & (back <= BAND)
            dist = (back * dilation).astype(F32)
            for hh in range(2):
                bias_ref[hh, :, off:off + width] = jnp.where(valid, -slopes_ref[2 * hp + hh] * dist, NEG)

    def load_unit(dilation, q_start, k_start, width):
        q = (q_ref[0, pl.ds(q_start, BAND, stride=dilation), :] * q_scale).astype(BF16)
        k = k_ref[0, pl.ds(k_start, width, stride=dilation), :].astype(BF16)
        v = v_ref[0, pl.ds(k_start, width, stride=dilation), :].astype(BF16)
        return q, k, v

    def scores(q, k, hh):
        width = k.shape[0]
        boff = 0 if width == BAND else BAND
        qm = jnp.where(head_masks[hh], q, jnp.zeros_like(q))
        s = lax.dot_general(qm, k, (((1,), (1,)), ((), ())), preferred_element_type=F32)
        return s + bias_ref[hh, :, boff:boff + width]

    def softmax(s):
        m = jnp.max(s, axis=-1, keepdims=True)
        return m, jnp.exp(s - m).astype(BF16)

    def weighted(p, v, hh):
        return jnp.dot(p, jnp.where(head_masks[hh], v, ones), preferred_element_type=F32)

    def run_group(branch, dilation, units):
        loaded = [load_unit(dilation, *u) for u in units]
        s_all = [[scores(q, k, hh) for hh in range(2)] for q, k, _ in loaded]
        mp_all = [[softmax(s) for s in s_u] for s_u in s_all]
        pv_all = [[weighted(mp[hh][1], v, hh) for hh in range(2)] for mp, (_, _, v) in zip(mp_all, loaded)]
        for (q_start, _, _), mp, pv in zip(units, mp_all, pv_all):
            num = jnp.where(head0, pv[0], pv[1])
            den = pltpu.roll(jnp.where(head0, pv[1], pv[0]), A_HEAD_DIM, 1)
            rows = pl.ds(q_start, BAND, stride=dilation)
            ob_ref[branch, rows, :] = num / den
            lse_ref[branch, rows, :] = jnp.where(head0, mp[0][0], mp[1][0]) + jnp.log(den)

    for branch, (_, dilation) in enumerate(DILATED_PATTERNS):
        n_blk = seq // dilation // BAND
        blk_stride = BAND * dilation
        fill_bias(dilation)
        first = lambda r: (r, r, BAND)
        later = lambda r, n, blk_stride=blk_stride: (n * blk_stride + r, (n - 1) * blk_stride + r, 2 * BAND)

        if n_blk == 1:
            group = DIL_GROUP_FIRST
            def body(g, c, branch=branch, dilation=dilation, group=group):
                run_group(branch, dilation, [first(g * group + u) for u in range(group)])
                return c
            lax.fori_loop(0, dilation // group, body, 0)
        elif dilation > 1:
            res = DIL_GROUP_RESIDUES
            def body(g, c, branch=branch, dilation=dilation, n_blk=n_blk, res=res):
                units = []
                for u in range(res):
                    r = g * res + u
                    units += [first(r)] + [later(r, n) for n in range(1, n_blk)]
                run_group(branch, dilation, units)
                return c
            lax.fori_loop(0, dilation // res, body, 0)
        else:
            per = DIL_GROUP_LATER
            assert (n_blk - per) % per == 0
            run_group(branch, dilation, [first(0)] + [later(0, n) for n in range(1, per)])

            def body(i, c, branch=branch, dilation=dilation, per=per):
                run_group(branch, dilation, [later(0, per * (i + 1) + u) for u in range(per)])
                return c
            lax.fori_loop(0, (n_blk - per) // per, body, 0)

    rows_per = 256

    def merge(i, carry):
        rows = pl.ds(pl.multiple_of(i * rows_per, rows_per), rows_per)
        l0, l1, l2 = lse_ref[0, rows, :], lse_ref[1, rows, :], lse_ref[2, rows, :]
        mx = jnp.maximum(jnp.maximum(l0, l1), l2)
        w0, w1, w2 = jnp.exp(l0 - mx), jnp.exp(l1 - mx), jnp.exp(l2 - mx)
        num = w0 * ob_ref[0, rows, :] + w1 * ob_ref[1, rows, :] + w2 * ob_ref[2, rows, :]
        o_ref[0, rows, :] = (num / (w0 + w1 + w2)).astype(o_ref.dtype)
        return carry

    lax.fori_loop(0, seq // rows_per, merge, 0)


def _dilated_attn(qkv, slopes):
    bsz, seq, _ = qkv.shape
    n_pairs = A_HEADS * A_HEAD_DIM // LANES
    assert all(w // d == BAND and seq % (d * BAND) == 0 for w, d in DILATED_PATTERNS)

    def col_spec(part):
        return pl.BlockSpec((1, seq, LANES), lambda b, h, part=part: (b, 0, part * n_pairs + h))

    return pl.pallas_call(
        functools.partial(_dilated_kernel, seq=seq),
        grid=(bsz, n_pairs),
        in_specs=[pl.BlockSpec(memory_space=pltpu.SMEM), col_spec(0), col_spec(1), col_spec(2)],
        out_specs=pl.BlockSpec((1, seq, LANES), lambda b, h: (b, 0, h)),
        out_shape=jax.ShapeDtypeStruct((bsz, seq, A_HEADS * A_HEAD_DIM), BF16),
        scratch_shapes=[pltpu.VMEM((3, seq, LANES), F32), pltpu.VMEM((3, seq, LANES), F32),
                        pltpu.VMEM((2, BAND, 3 * BAND), F32)],
        compiler_params=_params("parallel", "parallel"),
        name="dilated_attn",
    )(slopes, qkv, qkv, qkv)


def _proj_ln_kernel(o_ref, x_ref, w_ref, g_ref, b_ref, y_ref):
    mix = jnp.dot(o_ref[...], w_ref[...], preferred_element_type=F32)
    y_ref[...] = _layer_norm(ALPHA * x_ref[...] + mix, g_ref[...], b_ref[...])


def _proj_ln(o2d, x2d, w, g, b):
    t, d = x2d.shape
    k = o2d.shape[1]
    tm = 512
    return pl.pallas_call(
        _proj_ln_kernel,
        grid=(t // tm,),
        in_specs=[pl.BlockSpec((tm, k), lambda i: (i, 0)), pl.BlockSpec((tm, d), lambda i: (i, 0)),
                  _const_spec((k, d)), _const_spec((1, d)), _const_spec((1, d))],
        out_specs=pl.BlockSpec((tm, d), lambda i: (i, 0)),
        out_shape=jax.ShapeDtypeStruct((t, d), F32),
        compiler_params=_params("parallel"),
        name="proj_ln",
    )(o2d, x2d, w, g, b)


FFN_COLS = 256


def _ffn_kernel(x_ref, win_ref, cw_ref, cb_ref, wout_ref, g_ref, b_ref, y_ref, u_ref, h_ref,
                *, tm, tiles_per_seq):
    i = pl.program_id(0)
    halo = SUBLANES

    @pl.when(i % tiles_per_seq == 0)
    def _():
        u_ref[0:halo, :] = jnp.zeros((halo, 2 * D_FF), F32)

    x = x_ref[...]
    u_ref[halo:halo + tm, :] = jnp.dot(x.astype(BF16), win_ref[...], preferred_element_type=F32)

    for c in range(D_FF // FFN_COLS):
        def conv(col):
            cols = slice(col, col + FFN_COLS)
            acc = cb_ref[:, cols] + cw_ref[2:3, cols] * u_ref[halo:halo + tm, cols]
            acc = acc + cw_ref[1:2, cols] * u_ref[halo - 1:halo - 1 + tm, cols]
            return acc + cw_ref[0:1, cols] * u_ref[halo - 2:halo - 2 + tm, cols]

        gate = conv(c * FFN_COLS)
        val = conv(D_FF + c * FFN_COLS)
        h_ref[:, c * FFN_COLS:(c + 1) * FFN_COLS] = (jax.nn.silu(gate) * val).astype(BF16)

    u_ref[0:halo, :] = u_ref[tm:tm + halo, :]

    f = jnp.dot(h_ref[...], wout_ref[...], preferred_element_type=F32)
    y_ref[...] = _layer_norm(ALPHA * x + f, g_ref[...], b_ref[...])


def _conv_ffn_ln(x2d, seq, w_in, conv_w, conv_b, w_out, g, b):
    t, d = x2d.shape
    tm = 256
    assert seq % tm == 0 and D_FF % FFN_COLS == 0 and CONV_WIDTH - 1 <= SUBLANES
    return pl.pallas_call(
        functools.partial(_ffn_kernel, tm=tm, tiles_per_seq=seq // tm),
        grid=(t // tm,),
        in_specs=[pl.BlockSpec((tm, d), lambda i: (i, 0)), _const_spec((d, 2 * D_FF)),
                  _const_spec((CONV_WIDTH, 2 * D_FF)), _const_spec((1, 2 * D_FF)),
                  _const_spec((D_FF, d)), _const_spec((1, d)), _const_spec((1, d))],
        out_specs=pl.BlockSpec((tm, d), lambda i: (i, 0)),
        out_shape=jax.ShapeDtypeStruct((t, d), F32),
        scratch_shapes=[pltpu.VMEM((tm + SUBLANES, 2 * D_FF), F32), pltpu.VMEM((tm, D_FF), BF16)],
        compiler_params=_params("arbitrary"),
        name="conv_ffn_ln",
    )(x2d, w_in, conv_w, conv_b, w_out, g, b)


def _rope(t, cos, sin_signed):
    lane = lax.broadcasted_iota(jnp.int32, t.shape, 1)
    half = B_ROPE_DIM // 2
    first = (lane % B_ROPE_DIM) < half
    swapped = jnp.where(first, pltpu.roll(t, LANES - half, 1), pltpu.roll(t, half, 1))
    return t * cos + swapped * sin_signed


def _rms(c, g):
    return c * lax.rsqrt(jnp.mean(c * c, axis=-1, keepdims=True) + RMS_EPS) * g


def _mla_proj_kernel(x_ref, wd_ref, kvg_ref, qg_ref, wuk_ref, wuv_ref, wuqn_ref, wuqr_ref,
                     cos_ref, sin_ref, qn_ref, qr_ref, kn_ref, kr_ref, v_ref):
    x = x_ref[...].astype(BF16)
    down = jnp.dot(x, wd_ref[...], preferred_element_type=F32)
    cos, sin = cos_ref[...], sin_ref[...]

    c_kv = _rms(down[:, :KV_RANK], kvg_ref[...]).astype(BF16)
    kn_ref[...] = jnp.dot(c_kv, wuk_ref[...], preferred_element_type=F32).astype(BF16)
    v_ref[...] = jnp.dot(c_kv, wuv_ref[...], preferred_element_type=F32).astype(BF16)
    kr_ref[...] = _rope(down[:, KV_RANK:KV_RANK + LANES], cos, sin).astype(BF16)

    scale = (B_NOPE_DIM + B_ROPE_DIM) ** -0.5
    c_q = _rms(down[:, KV_RANK + LANES:], qg_ref[...]).astype(BF16)
    qn_ref[...] = (jnp.dot(c_q, wuqn_ref[...], preferred_element_type=F32) * scale).astype(BF16)
    q_rope = jnp.dot(c_q, wuqr_ref[...], preferred_element_type=F32)
    for c in range(q_rope.shape[1] // LANES):
        cols = slice(c * LANES, (c + 1) * LANES)
        qr_ref[:, cols] = (_rope(q_rope[:, cols], cos, sin) * scale).astype(BF16)


def _mla_proj(x2d, seq, w_down, kv_g, q_g, w_uk, w_uv, w_uqn, w_uqr, cos_t, sin_t):
    t, d = x2d.shape
    tm = 512
    n_seq_tiles = seq // tm
    hn, hr, hv = B_HEADS * B_NOPE_DIM, B_HEADS * B_ROPE_DIM, B_HEADS * B_V_DIM
    row = lambda n: pl.BlockSpec((tm, n), lambda i: (i, 0))
    tab = pl.BlockSpec((tm, LANES), lambda i: (i % n_seq_tiles, 0))
    return pl.pallas_call(
        _mla_proj_kernel,
        grid=(t // tm,),
        in_specs=[row(d), _const_spec(w_down.shape), _const_spec((1, KV_RANK)), _const_spec((1, Q_RANK)),
                  _const_spec(w_uk.shape), _const_spec(w_uv.shape), _const_spec(w_uqn.shape),
                  _const_spec(w_uqr.shape), tab, tab],
        out_specs=[row(hn), row(hr), row(hn), row(LANES), row(hv)],
        out_shape=[jax.ShapeDtypeStruct((t, hn), BF16), jax.ShapeDtypeStruct((t, hr), BF16),
                   jax.ShapeDtypeStruct((t, hn), BF16), jax.ShapeDtypeStruct((t, LANES), BF16),
                   jax.ShapeDtypeStruct((t, hv), BF16)],
        compiler_params=_params("parallel"),
        name="mla_proj",
    )(x2d, w_down, kv_g, q_g, w_uk, w_uv, w_uqn, w_uqr, cos_t, sin_t)


MLA_GROUP = 4
MLA_TQ = 256
MLA_TK = 256


def _mla_attn_kernel(qn_ref, qr_ref, kn_ref, kr_ref, v_ref, o_ref, qc_ref, m_ref, acc_ref):
    qi = pl.program_id(2)
    tq, tk = MLA_TQ, MLA_TK
    lane = lax.broadcasted_iota(jnp.int32, (1, LANES), 1)
    nt = (((1,), (1,)), ((), ()))
    ones = jnp.ones((1, LANES), BF16)

    qr_all = qr_ref[0]
    for h in range(MLA_GROUP):
        rows = slice((h % 2) * tq, (h % 2 + 1) * tq)
        qn_pair = qn_ref[0, :, (h // 2) * LANES:(h // 2 + 1) * LANES]
        qc_ref[h // 2, rows, :LANES] = jnp.where(lane // B_NOPE_DIM == h % 2, qn_pair, jnp.zeros_like(qn_pair))
        qc_ref[h // 2, rows, LANES:] = jnp.where(lane // B_ROPE_DIM == h, qr_all, jnp.zeros_like(qr_all))
    m_ref[...] = jnp.full(m_ref.shape, NEG, F32)
    acc_ref[...] = jnp.zeros(acc_ref.shape, F32)

    def block(ks, masked):
        kr = kr_ref[0, pl.ds(ks, tk), :]
        if masked:
            ql = lax.broadcasted_iota(jnp.int32, (2 * tq, tk), 0) & (tq - 1)
            kl = lax.broadcasted_iota(jnp.int32, (2 * tq, tk), 1)
            causal = kl <= ql
        pairs = range(MLA_GROUP // 2)
        s_all = []
        for pair in pairs:
            cols = slice(pair * LANES, (pair + 1) * LANES)
            kcat = jnp.concatenate([kn_ref[0, pl.ds(ks, tk), cols], kr], axis=1)
            s = lax.dot_general(qc_ref[pair], kcat, nt, preferred_element_type=F32)
            s_all.append(jnp.where(causal, s, NEG) if masked else s)
        mp_all = []
        for pair in pairs:
            s, m_old = s_all[pair], m_ref[pair]
            m_new = jnp.maximum(m_old, jnp.max(s, axis=-1, keepdims=True))
            p = jnp.concatenate([jnp.exp(s[:, c * LANES:(c + 1) * LANES] - m_new)
                                 for c in range(tk // LANES)], axis=1).astype(BF16)
            mp_all.append((m_old, m_new, p))
        for pair in pairs:
            m_old, m_new, p = mp_all[pair]
            v = v_ref[0, pl.ds(ks, tk), pair * LANES:(pair + 1) * LANES]
            pv = [jnp.dot(p[hh * tq:(hh + 1) * tq], jnp.where(lane // B_V_DIM == hh, v, ones),
                          preferred_element_type=F32) for hh in range(2)]
            acc_ref[pair] = jnp.exp(m_old - m_new) * acc_ref[pair] + jnp.concatenate(pv, axis=0)
            m_ref[pair] = m_new

    def body(j, carry):
        block(pl.multiple_of(j * tk, tk), False)
        return carry

    lax.fori_loop(0, qi, body, 0)
    block(pl.multiple_of(qi * tk, tk), True)

    first = lane < B_V_DIM
    for pair in range(MLA_GROUP // 2):
        a0, a1 = acc_ref[pair, :tq], acc_ref[pair, tq:]
        num = jnp.where(first, a0, a1)
        den = pltpu.roll(jnp.where(first, a1, a0), B_V_DIM, 1)
        o_ref[0, :, pair * LANES:(pair + 1) * LANES] = (num / den).astype(o_ref.dtype)


def _mla_attn(qn, qr, kn, kr, v):
    bsz, seq, _ = qn.shape
    n_groups = B_HEADS // MLA_GROUP
    gn, gr = MLA_GROUP * B_NOPE_DIM, MLA_GROUP * B_ROPE_DIM
    assert MLA_TQ == MLA_TK and seq % MLA_TQ == 0 and gr == LANES and B_NOPE_DIM == B_V_DIM
    return pl.pallas_call(
        _mla_attn_kernel,
        grid=(bsz, n_groups, seq // MLA_TQ),
        in_specs=[pl.BlockSpec((1, MLA_TQ, gn), lambda b, g, i: (b, i, g)),
                  pl.BlockSpec((1, MLA_TQ, gr), lambda b, g, i: (b, i, g)),
                  pl.BlockSpec((1, seq, gn), lambda b, g, i: (b, 0, g)),
                  pl.BlockSpec((1, seq, LANES), lambda b, g, i: (b, 0, 0)),
                  pl.BlockSpec((1, seq, gn), lambda b, g, i: (b, 0, g))],
        out_specs=pl.BlockSpec((1, MLA_TQ, gn), lambda b, g, i: (b, i, g)),
        out_shape=jax.ShapeDtypeStruct((bsz, seq, B_HEADS * B_V_DIM), BF16),
        scratch_shapes=[pltpu.VMEM((MLA_GROUP // 2, 2 * MLA_TQ, 2 * LANES), BF16),
                        pltpu.VMEM((MLA_GROUP // 2, 2 * MLA_TQ, LANES), F32),
                        pltpu.VMEM((MLA_GROUP // 2, 2 * MLA_TQ, LANES), F32)],
        compiler_params=_params("parallel", "parallel", "parallel"),
        name="mla_attn",
    )(qn, qr, kn, kr, v)


def _rope_tables(seq):
    inv_freq = ROPE_BASE ** (-jnp.arange(0, B_ROPE_DIM, 2, dtype=F32) / B_ROPE_DIM)
    ang = jnp.arange(seq, dtype=F32)[:, None] * inv_freq[None, :]
    cos, sin = jnp.cos(ang), jnp.sin(ang)
    reps = LANES // B_ROPE_DIM
    cos_t = jnp.tile(jnp.concatenate([cos, cos], axis=1), (1, reps))
    sin_t = jnp.tile(jnp.concatenate([-sin, sin], axis=1), (1, reps))
    return cos_t, sin_t


def kernel(x, a_w_qkv, a_w_o, kv_w_dkv, kv_norm_g, kv_w_kr, kv_w_uk, kv_w_uv, b_w_dq, b_q_norm_g, b_w_uq, b_w_o, ffn_w_in, ffn_conv_w, ffn_conv_b, ffn_w_out, ln_mix_g, ln_mix_b, ln_ffn_g, ln_ffn_b):
    bsz, seq, d = x.shape
    t = bsz * seq
    x2d = x.reshape(t, d)
    row = lambda a: a.reshape(1, -1)

    def ffn(xin, layer):
        return _conv_ffn_ln(xin, seq, ffn_w_in[layer].astype(BF16), ffn_conv_w[layer], row(ffn_conv_b[layer]),
                            ffn_w_out[layer].astype(BF16), row(ln_ffn_g[layer]), row(ln_ffn_b[layer]))

    slopes = jnp.asarray([2.0 ** (-8.0 * (h + 1) / A_HEADS) for h in range(A_HEADS)], dtype=F32)
    qkv = _qkv_proj(x2d, a_w_qkv[0].astype(BF16))
    o = _dilated_attn(qkv.reshape(bsz, seq, -1), slopes)
    x2d = _proj_ln(o.reshape(t, -1), x2d, a_w_o[0].astype(BF16), row(ln_mix_g[0]), row(ln_mix_b[0]))
    x2d = ffn(x2d, 0)

    cos_t, sin_t = _rope_tables(seq)
    w_down = jnp.concatenate([kv_w_dkv, jnp.tile(kv_w_kr, (1, LANES // B_ROPE_DIM)), b_w_dq[0]], axis=1)
    w_uq = b_w_uq[0].reshape(Q_RANK, B_HEADS, B_NOPE_DIM + B_ROPE_DIM)
    w_uqn = w_uq[:, :, :B_NOPE_DIM].reshape(Q_RANK, -1)
    w_uqr = w_uq[:, :, B_NOPE_DIM:].reshape(Q_RANK, -1)
    qn, qr, kn, kr, v = _mla_proj(x2d, seq, w_down.astype(BF16), row(kv_norm_g), row(b_q_norm_g[0]),
                                  kv_w_uk.astype(BF16), kv_w_uv.astype(BF16), w_uqn.astype(BF16),
                                  w_uqr.astype(BF16), cos_t, sin_t)
    shp = lambda a: a.reshape(bsz, seq, -1)
    o = _mla_attn(shp(qn), shp(qr), shp(kn), shp(kr), shp(v))
    x2d = _proj_ln(o.reshape(t, -1), x2d, b_w_o[0].astype(BF16), row(ln_mix_g[1]), row(ln_mix_b[1]))
    x2d = ffn(x2d, 1)
    return x2d.reshape(bsz, seq, d)
```

```python
import functools

import jax
import jax.numpy as jnp
from jax import lax
from jax.experimental import pallas as pl
from jax.experimental.pallas import tpu as pltpu

D_MODEL = 1024
DEPTH = 2
A_HEADS = 16
A_HEAD_DIM = 64
DILATED_PATTERNS = ((128, 1), (512, 4), (2048, 16))
BAND = 128
B_HEADS = 16
B_NOPE_DIM = 64
B_ROPE_DIM = 32
B_V_DIM = 64
Q_RANK = 384
KV_RANK = 256
ROPE_BASE = 10000.0
D_FF = 2816
CONV_WIDTH = 3
ALPHA = (2.0 * DEPTH) ** 0.25
LN_EPS = 1e-5
RMS_EPS = 1e-6

LANES = 128
SUBLANES = 8
NEG = -1e30
LOG2E = 1.4426950408889634
VMEM_LIMIT = 56 * 1024 * 1024

F32 = jnp.float32
BF16 = jnp.bfloat16


def _const_spec(shape):
    nd = len(shape)
    return pl.BlockSpec(shape, lambda *_: (0,) * nd, pipeline_mode=pl.Buffered(1))


def _params(*sem):
    return pltpu.CompilerParams(dimension_semantics=sem, vmem_limit_bytes=VMEM_LIMIT)


def _layer_norm(y, g, b):
    mu = jnp.mean(y, axis=-1, keepdims=True)
    yc = y - mu
    var = jnp.mean(yc * yc, axis=-1, keepdims=True)
    return yc * lax.rsqrt(var + LN_EPS) * g + b


def _qkv_kernel(x_ref, w_ref, o_ref):
    o_ref[...] = jnp.dot(x_ref[...].astype(BF16), w_ref[...], preferred_element_type=F32)


def _qkv_proj(x2d, w):
    t, d = x2d.shape
    n = w.shape[1]
    tm = 512
    return pl.pallas_call(
        _qkv_kernel,
        grid=(t // tm,),
        in_specs=[pl.BlockSpec((tm, d), lambda i: (i, 0)), _const_spec((d, n))],
        out_specs=pl.BlockSpec((tm, n), lambda i: (i, 0)),
        out_shape=jax.ShapeDtypeStruct((t, n), F32),
        compiler_params=_params("parallel"),
        name="qkv_proj",
    )(x2d, w)


DIL_GROUP_FIRST = 8
DIL_GROUP_RESIDUES = 2
DIL_GROUP_LATER = 4


def _dilated_kernel(slopes_ref, q_ref, k_ref, v_ref, o_ref, ob_ref, lse_ref, bias_ref, *, seq):
    hp = pl.program_id(1)
    lane = lax.broadcasted_iota(jnp.int32, (1, LANES), 1)
    head0 = lane < A_HEAD_DIM
    head_masks = (head0, jnp.logical_not(head0))
    q_scale = A_HEAD_DIM ** -0.5 * LOG2E
    ones = jnp.ones((1, LANES), BF16)

    def fill_bias(dilation):
        for off, width in ((0, BAND), (BAND, 2 * BAND)):
            ql = lax.broadcasted_iota(jnp.int32, (BAND, width), 0)
            kl = lax.broadcasted_iota(jnp.int32, (BAND, width), 1)
            back = ql - kl + (width - BAND)
            valid = (back >= 0) & (back <= BAND)
            dist = (back * dilation).astype(F32)
            for hh in range(2):
                slope = slopes_ref[2 * hp + hh] * LOG2E
                bias_ref[hh, :, off:off + width] = jnp.where(valid, -slope * dist, NEG)

    def load_unit(dilation, q_start, k_start, width):
        q = (q_ref[0, pl.ds(q_start, BAND, stride=dilation), :] * q_scale).astype(BF16)
        k = k_ref[0, pl.ds(k_start, width, stride=dilation), :].astype(BF16)
        v = v_ref[0, pl.ds(k_start, width, stride=dilation), :].astype(BF16)
        return q, k, v

    def scores(q, k, hh):
        width = k.shape[0]
        boff = 0 if width == BAND else BAND
        qm = jnp.where(head_masks[hh], q, jnp.zeros_like(q))
        s = lax.dot_general(qm, k, (((1,), (1,)), ((), ())), preferred_element_type=F32)
        return s + bias_ref[hh, :, boff:boff + width]

    def softmax(s):
        m = jnp.max(s, axis=-1, keepdims=True)
        return m, jnp.exp2(s - m).astype(BF16)

    def weighted(p, v, hh):
        return jnp.dot(p, jnp.where(head_masks[hh], v, ones), preferred_element_type=F32)

    def run_group(branch, dilation, units):
        loaded = [load_unit(dilation, *u) for u in units]
        s_all = [[scores(q, k, hh) for hh in range(2)] for q, k, _ in loaded]
        mp_all = [[softmax(s) for s in s_u] for s_u in s_all]
        pv_all = [[weighted(mp[hh][1], v, hh) for hh in range(2)] for mp, (_, _, v) in zip(mp_all, loaded)]
        for (q_start, _, _), mp, pv in zip(units, mp_all, pv_all):
            num = jnp.where(head0, pv[0], pv[1])
            den = pltpu.roll(jnp.where(head0, pv[1], pv[0]), A_HEAD_DIM, 1)
            rows = pl.ds(q_start, BAND, stride=dilation)
            ob_ref[branch, rows, :] = num / den
            lse_ref[branch, rows, :] = jnp.where(head0, mp[0][0], mp[1][0]) + jnp.log2(den)

    for branch, (_, dilation) in enumerate(DILATED_PATTERNS):
        n_blk = seq // dilation // BAND
        blk_stride = BAND * dilation
        fill_bias(dilation)
        first = lambda r: (r, r, BAND)
        later = lambda r, n, blk_stride=blk_stride: (n * blk_stride + r, (n - 1) * blk_stride + r, 2 * BAND)

        if n_blk == 1:
            group = DIL_GROUP_FIRST
            def body(g, c, branch=branch, dilation=dilation, group=group):
                run_group(branch, dilation, [first(g * group + u) for u in range(group)])
                return c
            lax.fori_loop(0, dilation // group, body, 0)
        elif dilation > 1:
            res = DIL_GROUP_RESIDUES
            def body(g, c, branch=branch, dilation=dilation, n_blk=n_blk, res=res):
                units = []
                for u in range(res):
                    r = g * res + u
                    units += [first(r)] + [later(r, n) for n in range(1, n_blk)]
                run_group(branch, dilation, units)
                return c
            lax.fori_loop(0, dilation // res, body, 0)
        else:
            per = DIL_GROUP_LATER
            assert (n_blk - per) % per == 0
            run_group(branch, dilation, [first(0)] + [later(0, n) for n in range(1, per)])

            def body(i, c, branch=branch, dilation=dilation, per=per):
                run_group(branch, dilation, [later(0, per * (i + 1) + u) for u in range(per)])
                return c
            lax.fori_loop(0, (n_blk - per) // per, body, 0)

    rows_per = 256

    def merge(i, carry):
        rows = pl.ds(pl.multiple_of(i * rows_per, rows_per), rows_per)
        l0, l1, l2 = lse_ref[0, rows, :], lse_ref[1, rows, :], lse_ref[2, rows, :]
        mx = jnp.maximum(jnp.maximum(l0, l1), l2)
        w0, w1, w2 = jnp.exp2(l0 - mx), jnp.exp2(l1 - mx), jnp.exp2(l2 - mx)
        num = w0 * ob_ref[0, rows, :] + w1 * ob_ref[1, rows, :] + w2 * ob_ref[2, rows, :]
        o_ref[0, rows, :] = (num / (w0 + w1 + w2)).astype(o_ref.dtype)
        return carry

    lax.fori_loop(0, seq // rows_per, merge, 0)


def _dilated_attn(qkv, slopes):
    bsz, seq, _ = qkv.shape
    n_pairs = A_HEADS * A_HEAD_DIM // LANES
    assert all(w // d == BAND and seq % (d * BAND) == 0 for w, d in DILATED_PATTERNS)

    def col_spec(part):
        return pl.BlockSpec((1, seq, LANES), lambda b, h, part=part: (b, 0, part * n_pairs + h))

    return pl.pallas_call(
        functools.partial(_dilated_kernel, seq=seq),
        grid=(bsz, n_pairs),
        in_specs=[pl.BlockSpec(memory_space=pltpu.SMEM), col_spec(0), col_spec(1), col_spec(2)],
        out_specs=pl.BlockSpec((1, seq, LANES), lambda b, h: (b, 0, h)),
        out_shape=jax.ShapeDtypeStruct((bsz, seq, A_HEADS * A_HEAD_DIM), BF16),
        scratch_shapes=[pltpu.VMEM((3, seq, LANES), F32), pltpu.VMEM((3, seq, LANES), F32),
                        pltpu.VMEM((2, BAND, 3 * BAND), F32)],
        compiler_params=_params("parallel", "parallel"),
        name="dilated_attn",
    )(slopes, qkv, qkv, qkv)


FFN_COLS = 256


def _tail_kernel(o_ref, x_ref, wo_ref, gm_ref, bm_ref, win_ref, cw_ref, cb_ref, wout_ref, gf_ref, bf_ref,
                 y_ref, u_ref, h_ref, *, tm, tiles_per_seq):
    i = pl.program_id(0)
    halo = SUBLANES

    @pl.when(i % tiles_per_seq == 0)
    def _():
        u_ref[0:halo, :] = jnp.zeros((halo, 2 * D_FF), F32)

    mix = jnp.dot(o_ref[...], wo_ref[...], preferred_element_type=F32)
    x = _layer_norm(ALPHA * x_ref[...] + mix, gm_ref[...], bm_ref[...])
    u_ref[halo:halo + tm, :] = jnp.dot(x.astype(BF16), win_ref[...], preferred_element_type=F32)

    for c in range(D_FF // FFN_COLS):
        def conv(col):
            cols = slice(col, col + FFN_COLS)
            acc = cb_ref[:, cols] + cw_ref[2:3, cols] * u_ref[halo:halo + tm, cols]
            acc = acc + cw_ref[1:2, cols] * u_ref[halo - 1:halo - 1 + tm, cols]
            return acc + cw_ref[0:1, cols] * u_ref[halo - 2:halo - 2 + tm, cols]

        gate = conv(c * FFN_COLS)
        val = conv(D_FF + c * FFN_COLS)
        h_ref[:, c * FFN_COLS:(c + 1) * FFN_COLS] = (jax.nn.silu(gate) * val).astype(BF16)

    u_ref[0:halo, :] = u_ref[tm:tm + halo, :]

    f = jnp.dot(h_ref[...], wout_ref[...], preferred_element_type=F32)
    y_ref[...] = _layer_norm(ALPHA * x + f, gf_ref[...], bf_ref[...])


def _layer_tail(o2d, x2d, seq, w_o, g_mix, b_mix, w_in, conv_w, conv_b, w_out, g_ffn, b_ffn):
    t, d = x2d.shape
    k = o2d.shape[1]
    tm = 256
    assert seq % tm == 0 and D_FF % FFN_COLS == 0 and CONV_WIDTH - 1 <= SUBLANES
    return pl.pallas_call(
        functools.partial(_tail_kernel, tm=tm, tiles_per_seq=seq // tm),
        grid=(t // tm,),
        in_specs=[pl.BlockSpec((tm, k), lambda i: (i, 0)), pl.BlockSpec((tm, d), lambda i: (i, 0)),
                  _const_spec((k, d)), _const_spec((1, d)), _const_spec((1, d)),
                  _const_spec((d, 2 * D_FF)), _const_spec((CONV_WIDTH, 2 * D_FF)), _const_spec((1, 2 * D_FF)),
                  _const_spec((D_FF, d)), _const_spec((1, d)), _const_spec((1, d))],
        out_specs=pl.BlockSpec((tm, d), lambda i: (i, 0)),
        out_shape=jax.ShapeDtypeStruct((t, d), F32),
        scratch_shapes=[pltpu.VMEM((tm + SUBLANES, 2 * D_FF), F32), pltpu.VMEM((tm, D_FF), BF16)],
        compiler_params=_params("arbitrary"),
        name="layer_tail",
    )(o2d, x2d, w_o, g_mix, b_mix, w_in, conv_w, conv_b, w_out, g_ffn, b_ffn)


def _rope(t, cos, sin_signed):
    lane = lax.broadcasted_iota(jnp.int32, t.shape, 1)
    half = B_ROPE_DIM // 2
    first = (lane % B_ROPE_DIM) < half
    swapped = jnp.where(first, pltpu.roll(t, LANES - half, 1), pltpu.roll(t, half, 1))
    return t * cos + swapped * sin_signed


def _rms(c, g):
    return c * lax.rsqrt(jnp.mean(c * c, axis=-1, keepdims=True) + RMS_EPS) * g


def _mla_proj_kernel(x_ref, wd_ref, kvg_ref, qg_ref, wuk_ref, wuv_ref, wuqn_ref, wuqr_ref,
                     cos_ref, sin_ref, qn_ref, qr_ref, kn_ref, kr_ref, v_ref):
    x = x_ref[...].astype(BF16)
    down = jnp.dot(x, wd_ref[...], preferred_element_type=F32)
    cos, sin = cos_ref[...], sin_ref[...]

    c_kv = _rms(down[:, :KV_RANK], kvg_ref[...]).astype(BF16)
    kn_t = lax.dot_general(wuk_ref[...], c_kv, (((1,), (1,)), ((), ())), preferred_element_type=F32)
    kr_t = _rope(down[:, KV_RANK:KV_RANK + LANES], cos, sin).T
    for c in range(kn_t.shape[1] // MLA_TK):
        cols = slice(c * MLA_TK, (c + 1) * MLA_TK)
        kn_ref[0, c] = kn_t[:, cols].astype(BF16)
        kr_ref[0, c] = kr_t[:, cols].astype(BF16)
    v_ref[...] = jnp.dot(c_kv, wuv_ref[...], preferred_element_type=F32).astype(BF16)

    scale = (B_NOPE_DIM + B_ROPE_DIM) ** -0.5 * LOG2E
    c_q = _rms(down[:, KV_RANK + LANES:], qg_ref[...]).astype(BF16)
    qn_ref[...] = (jnp.dot(c_q, wuqn_ref[...], preferred_element_type=F32) * scale).astype(BF16)
    q_rope = jnp.dot(c_q, wuqr_ref[...], preferred_element_type=F32)
    for c in range(q_rope.shape[1] // LANES):
        cols = slice(c * LANES, (c + 1) * LANES)
        qr_ref[:, cols] = (_rope(q_rope[:, cols], cos, sin) * scale).astype(BF16)


def _mla_proj(x2d, seq, w_down, kv_g, q_g, w_uk_t, w_uv, w_uqn, w_uqr, cos_t, sin_t):
    t, d = x2d.shape
    tm = 512
    n_seq_tiles = seq // tm
    bsz = t // seq
    blk = tm // MLA_TK
    hn, hr, hv = B_HEADS * B_NOPE_DIM, B_HEADS * B_ROPE_DIM, B_HEADS * B_V_DIM
    row = lambda n: pl.BlockSpec((tm, n), lambda i: (i, 0))
    tab = pl.BlockSpec((tm, LANES), lambda i: (i % n_seq_tiles, 0))
    key_t = lambda n: pl.BlockSpec((1, blk, n, MLA_TK), lambda i: (i // n_seq_tiles, i % n_seq_tiles, 0, 0))
    return pl.pallas_call(
        _mla_proj_kernel,
        grid=(t // tm,),
        in_specs=[row(d), _const_spec(w_down.shape), _const_spec((1, KV_RANK)), _const_spec((1, Q_RANK)),
                  _const_spec(w_uk_t.shape), _const_spec(w_uv.shape), _const_spec(w_uqn.shape),
                  _const_spec(w_uqr.shape), tab, tab],
        out_specs=[row(hn), row(hr), key_t(hn), key_t(LANES), row(hv)],
        out_shape=[jax.ShapeDtypeStruct((t, hn), BF16), jax.ShapeDtypeStruct((t, hr), BF16),
                   jax.ShapeDtypeStruct((bsz, seq // MLA_TK, hn, MLA_TK), BF16),
                   jax.ShapeDtypeStruct((bsz, seq // MLA_TK, LANES, MLA_TK), BF16),
                   jax.ShapeDtypeStruct((t, hv), BF16)],
        compiler_params=_params("parallel"),
        name="mla_proj",
    )(x2d, w_down, kv_g, q_g, w_uk_t, w_uv, w_uqn, w_uqr, cos_t, sin_t)


MLA_GROUP = 4
MLA_TQ = 256
MLA_TK = 256


def _mla_attn_kernel(qn_ref, qr_ref, kn_ref, kr_ref, v_ref, o_ref, qc_ref, s0_ref, s1_ref, m_ref, acc_ref):
    qi = pl.program_id(2)
    tq, tk = MLA_TQ, MLA_TK
    lane = lax.broadcasted_iota(jnp.int32, (1, LANES), 1)
    ones = jnp.ones((1, LANES), BF16)
    pairs = range(MLA_GROUP // 2)

    qr_all = qr_ref[0]
    for h in range(MLA_GROUP):
        rows = slice((h % 2) * tq, (h % 2 + 1) * tq)
        qn_pair = qn_ref[0, :, (h // 2) * LANES:(h // 2 + 1) * LANES]
        qc_ref[h // 2, rows, :LANES] = jnp.where(lane // B_NOPE_DIM == h % 2, qn_pair, jnp.zeros_like(qn_pair))
        qc_ref[h // 2, rows, LANES:] = jnp.where(lane // B_ROPE_DIM == h, qr_all, jnp.zeros_like(qr_all))
    m_ref[...] = jnp.full(m_ref.shape, NEG, F32)
    acc_ref[...] = jnp.zeros(acc_ref.shape, F32)

    def scores(j, s_ref):
        kr_t = kr_ref[0, j]
        for pair in pairs:
            k_t = jnp.concatenate([kn_ref[0, j, pair * LANES:(pair + 1) * LANES, :], kr_t], axis=0)
            s_ref[pair] = jnp.dot(qc_ref[pair], k_t, preferred_element_type=F32)

    def absorb(j, s_ref, masked):
        if masked:
            ql = lax.broadcasted_iota(jnp.int32, (2 * tq, tk), 0) & (tq - 1)
            kl = lax.broadcasted_iota(jnp.int32, (2 * tq, tk), 1)
            causal = kl <= ql
        mp_all = []
        for pair in pairs:
            s, m_old = s_ref[pair], m_ref[pair]
            if masked:
                s = jnp.where(causal, s, NEG)
            m_new = jnp.maximum(m_old, jnp.max(s, axis=-1, keepdims=True))
            p = jnp.concatenate([jnp.exp2(s[:, c * LANES:(c + 1) * LANES] - m_new)
                                 for c in range(tk // LANES)], axis=1).astype(BF16)
            mp_all.append((m_old, m_new, p))
        ks = pl.multiple_of(j * tk, tk)
        for pair in pairs:
            m_old, m_new, p = mp_all[pair]
            v = v_ref[0, pl.ds(ks, tk), pair * LANES:(pair + 1) * LANES]
            pv = [jnp.dot(p[hh * tq:(hh + 1) * tq], jnp.where(lane // B_V_DIM == hh, v, ones),
                          preferred_element_type=F32) for hh in range(2)]
            acc_ref[pair] = jnp.exp2(m_old - m_new) * acc_ref[pair] + jnp.concatenate(pv, axis=0)
            m_ref[pair] = m_new

    scores(0, s0_ref)

    def two_blocks(i, carry):
        j = 2 * i
        scores(j + 1, s1_ref)
        absorb(j, s0_ref, False)
        scores(j + 2, s0_ref)
        absorb(j + 1, s1_ref, False)
        return carry

    lax.fori_loop(0, qi // 2, two_blocks, 0)

    @pl.when(qi % 2 == 0)
    def _():
        absorb(qi, s0_ref, True)

    @pl.when(qi % 2 == 1)
    def _():
        scores(qi, s1_ref)
        absorb(qi - 1, s0_ref, False)
        absorb(qi, s1_ref, True)

    first = lane < B_V_DIM
    for pair in pairs:
        a0, a1 = acc_ref[pair, :tq], acc_ref[pair, tq:]
        num = jnp.where(first, a0, a1)
        den = pltpu.roll(jnp.where(first, a1, a0), B_V_DIM, 1)
        o_ref[0, :, pair * LANES:(pair + 1) * LANES] = (num / den).astype(o_ref.dtype)


def _mla_attn(qn, qr, kn_t, kr_t, v):
    bsz, seq, _ = qn.shape
    n_groups = B_HEADS // MLA_GROUP
    n_kblk = seq // MLA_TK
    gn, gr = MLA_GROUP * B_NOPE_DIM, MLA_GROUP * B_ROPE_DIM
    assert MLA_TQ == MLA_TK and seq % MLA_TQ == 0 and gr == LANES and B_NOPE_DIM == B_V_DIM
    score_buf = pltpu.VMEM((MLA_GROUP // 2, 2 * MLA_TQ, MLA_TK), F32)
    return pl.pallas_call(
        _mla_attn_kernel,
        grid=(bsz, n_groups, seq // MLA_TQ),
        in_specs=[pl.BlockSpec((1, MLA_TQ, gn), lambda b, g, i: (b, i, g)),
                  pl.BlockSpec((1, MLA_TQ, gr), lambda b, g, i: (b, i, g)),
                  pl.BlockSpec((1, n_kblk, gn, MLA_TK), lambda b, g, i: (b, 0, g, 0)),
                  pl.BlockSpec((1, n_kblk, LANES, MLA_TK), lambda b, g, i: (b, 0, 0, 0)),
                  pl.BlockSpec((1, seq, gn), lambda b, g, i: (b, 0, g))],
        out_specs=pl.BlockSpec((1, MLA_TQ, gn), lambda b, g, i: (b, i, g)),
        out_shape=jax.ShapeDtypeStruct((bsz, seq, B_HEADS * B_V_DIM), BF16),
        scratch_shapes=[pltpu.VMEM((MLA_GROUP // 2, 2 * MLA_TQ, 2 * LANES), BF16), score_buf, score_buf,
                        pltpu.VMEM((MLA_GROUP // 2, 2 * MLA_TQ, LANES), F32),
                        pltpu.VMEM((MLA_GROUP // 2, 2 * MLA_TQ, LANES), F32)],
        compiler_params=_params("parallel", "parallel", "parallel"),
        name="mla_attn",
    )(qn, qr, kn_t, kr_t, v)


def _rope_tables(seq):
    inv_freq = ROPE_BASE ** (-jnp.arange(0, B_ROPE_DIM, 2, dtype=F32) / B_ROPE_DIM)
    ang = jnp.arange(seq, dtype=F32)[:, None] * inv_freq[None, :]
    cos, sin = jnp.cos(ang), jnp.sin(ang)
    reps = LANES // B_ROPE_DIM
    cos_t = jnp.tile(jnp.concatenate([cos, cos], axis=1), (1, reps))
    sin_t = jnp.tile(jnp.concatenate([-sin, sin], axis=1), (1, reps))
    return cos_t, sin_t


def kernel(x, a_w_qkv, a_w_o, kv_w_dkv, kv_norm_g, kv_w_kr, kv_w_uk, kv_w_uv, b_w_dq, b_q_norm_g, b_w_uq, b_w_o, ffn_w_in, ffn_conv_w, ffn_conv_b, ffn_w_out, ln_mix_g, ln_mix_b, ln_ffn_g, ln_ffn_b):
    bsz, seq, d = x.shape
    t = bsz * seq
    x2d = x.reshape(t, d)
    row = lambda a: a.reshape(1, -1)

    def tail(o, xin, w_o, layer):
        return _layer_tail(o.reshape(t, -1), xin, seq, w_o.astype(BF16), row(ln_mix_g[layer]), row(ln_mix_b[layer]),
                           ffn_w_in[layer].astype(BF16), ffn_conv_w[layer], row(ffn_conv_b[layer]),
                           ffn_w_out[layer].astype(BF16), row(ln_ffn_g[layer]), row(ln_ffn_b[layer]))

    slopes = jnp.asarray([2.0 ** (-8.0 * (h + 1) / A_HEADS) for h in range(A_HEADS)], dtype=F32)
    qkv = _qkv_proj(x2d, a_w_qkv[0].astype(BF16))
    o = _dilated_attn(qkv.reshape(bsz, seq, -1), slopes)
    x2d = tail(o, x2d, a_w_o[0], 0)

    cos_t, sin_t = _rope_tables(seq)
    w_down = jnp.concatenate([kv_w_dkv, jnp.tile(kv_w_kr, (1, LANES // B_ROPE_DIM)), b_w_dq[0]], axis=1)
    w_uq = b_w_uq[0].reshape(Q_RANK, B_HEADS, B_NOPE_DIM + B_ROPE_DIM)
    w_uqn = w_uq[:, :, :B_NOPE_DIM].reshape(Q_RANK, -1)
    w_uqr = w_uq[:, :, B_NOPE_DIM:].reshape(Q_RANK, -1)
    qn, qr, kn_t, kr_t, v = _mla_proj(x2d, seq, w_down.astype(BF16), row(kv_norm_g), row(b_q_norm_g[0]),
                                      kv_w_uk.T.astype(BF16), kv_w_uv.astype(BF16), w_uqn.astype(BF16),
                                      w_uqr.astype(BF16), cos_t, sin_t)
    shp = lambda a: a.reshape(bsz, seq, -1)
    o = _mla_attn(shp(qn), shp(qr), kn_t, kr_t, shp(v))
    x2d = tail(o, x2d, b_w_o[0], 1)
    return x2d.reshape(bsz, seq, d)
```

```python
import functools

import jax
import jax.numpy as jnp
from jax import lax
from jax.experimental import pallas as pl
from jax.experimental.pallas import tpu as pltpu

D_MODEL = 1024
DEPTH = 2
A_HEADS = 16
A_HEAD_DIM = 64
DILATED_PATTERNS = ((128, 1), (512, 4), (2048, 16))
BAND = 128
B_HEADS = 16
B_NOPE_DIM = 64
B_ROPE_DIM = 32
B_V_DIM = 64
Q_RANK = 384
KV_RANK = 256
ROPE_BASE = 10000.0
D_FF = 2816
CONV_WIDTH = 3
ALPHA = (2.0 * DEPTH) ** 0.25
LN_EPS = 1e-5
RMS_EPS = 1e-6

LANES = 128
SUBLANES = 8
NEG = -1e30
LOG2E = 1.4426950408889634
VMEM_LIMIT = 56 * 1024 * 1024

F32 = jnp.float32
BF16 = jnp.bfloat16


def _const_spec(shape):
    nd = len(shape)
    return pl.BlockSpec(shape, lambda *_: (0,) * nd, pipeline_mode=pl.Buffered(1))


def _params(*sem):
    return pltpu.CompilerParams(dimension_semantics=sem, vmem_limit_bytes=VMEM_LIMIT)


def _layer_norm(y, g, b):
    mu = jnp.mean(y, axis=-1, keepdims=True)
    yc = y - mu
    var = jnp.mean(yc * yc, axis=-1, keepdims=True)
    return yc * lax.rsqrt(var + LN_EPS) * g + b


def _qkv_kernel(x_ref, w_ref, o_ref):
    o_ref[...] = jnp.dot(x_ref[...].astype(BF16), w_ref[...], preferred_element_type=F32)


def _qkv_proj(x2d, w):
    t, d = x2d.shape
    n = w.shape[1]
    tm = 512
    return pl.pallas_call(
        _qkv_kernel,
        grid=(t // tm,),
        in_specs=[pl.BlockSpec((tm, d), lambda i: (i, 0)), _const_spec((d, n))],
        out_specs=pl.BlockSpec((tm, n), lambda i: (i, 0)),
        out_shape=jax.ShapeDtypeStruct((t, n), F32),
        compiler_params=_params("parallel"),
        name="qkv_proj",
    )(x2d, w)


DIL_GROUP_FIRST = 8
DIL_GROUP_RESIDUES = 2
DIL_GROUP_LATER = 4


def _dilated_kernel(slopes_ref, q_ref, k_ref, v_ref, o_ref, ob_ref, lse_ref, bias_ref, *, seq):
    hp = pl.program_id(1)
    lane = lax.broadcasted_iota(jnp.int32, (1, LANES), 1)
    head0 = lane < A_HEAD_DIM
    head_masks = (head0, jnp.logical_not(head0))
    q_scale = A_HEAD_DIM ** -0.5 * LOG2E
    ones = jnp.ones((1, LANES), BF16)

    def fill_bias(dilation):
        for off, width in ((0, BAND), (BAND, 2 * BAND)):
            ql = lax.broadcasted_iota(jnp.int32, (BAND, width), 0)
            kl = lax.broadcasted_iota(jnp.int32, (BAND, width), 1)
            back = ql - kl + (width - BAND)
            valid = (back >= 0) & (back <= BAND)
            dist = (back * dilation).astype(F32)
            for hh in range(2):
                slope = slopes_ref[2 * hp + hh] * LOG2E
                bias_ref[hh, :, off:off + width] = jnp.where(valid, -slope * dist, NEG)

    def load_unit(dilation, q_start, k_start, width):
        q = (q_ref[0, pl.ds(q_start, BAND, stride=dilation), :] * q_scale).astype(BF16)
        k = k_ref[0, pl.ds(k_start, width, stride=dilation), :].astype(BF16)
        v = v_ref[0, pl.ds(k_start, width, stride=dilation), :].astype(BF16)
        return q, k, v

    def scores(q, k):
        width = k.shape[0]
        boff = 0 if width == BAND else BAND
        qs = jnp.concatenate([jnp.where(mask, q, jnp.zeros_like(q)) for mask in head_masks], axis=0)
        s = lax.dot_general(qs, k, (((1,), (1,)), ((), ())), preferred_element_type=F32)
        return [s[hh * BAND:(hh + 1) * BAND] + bias_ref[hh, :, boff:boff + width] for hh in range(2)]

    def softmax(s):
        m = jnp.max(s, axis=-1, keepdims=True)
        return m, jnp.exp2(s - m).astype(BF16)

    def weighted(p, v, hh):
        return jnp.dot(p, jnp.where(head_masks[hh], v, ones), preferred_element_type=F32)

    def run_group(branch, dilation, units):
        loaded = [load_unit(dilation, *u) for u in units]
        s_all = [scores(q, k) for q, k, _ in loaded]
        mp_all = [[softmax(s) for s in s_u] for s_u in s_all]
        pv_all = [[weighted(mp[hh][1], v, hh) for hh in range(2)] for mp, (_, _, v) in zip(mp_all, loaded)]
        for (q_start, _, _), mp, pv in zip(units, mp_all, pv_all):
            num = jnp.where(head0, pv[0], pv[1])
            den = pltpu.roll(jnp.where(head0, pv[1], pv[0]), A_HEAD_DIM, 1)
            rows = pl.ds(q_start, BAND, stride=dilation)
            ob_ref[branch, rows, :] = num / den
            lse_ref[branch, rows, :] = jnp.where(head0, mp[0][0], mp[1][0]) + jnp.log2(den)

    for branch, (_, dilation) in enumerate(DILATED_PATTERNS):
        n_blk = seq // dilation // BAND
        blk_stride = BAND * dilation
        fill_bias(dilation)
        first = lambda r: (r, r, BAND)
        later = lambda r, n, blk_stride=blk_stride: (n * blk_stride + r, (n - 1) * blk_stride + r, 2 * BAND)

        if n_blk == 1:
            group = DIL_GROUP_FIRST
            def body(g, c, branch=branch, dilation=dilation, group=group):
                run_group(branch, dilation, [first(g * group + u) for u in range(group)])
                return c
            lax.fori_loop(0, dilation // group, body, 0)
        elif dilation > 1:
            res = DIL_GROUP_RESIDUES
            def body(g, c, branch=branch, dilation=dilation, n_blk=n_blk, res=res):
                units = []
                for u in range(res):
                    r = g * res + u
                    units += [first(r)] + [later(r, n) for n in range(1, n_blk)]
                run_group(branch, dilation, units)
                return c
            lax.fori_loop(0, dilation // res, body, 0)
        else:
            per = DIL_GROUP_LATER
            assert (n_blk - per) % per == 0
            run_group(branch, dilation, [first(0)] + [later(0, n) for n in range(1, per)])

            def body(i, c, branch=branch, dilation=dilation, per=per):
                run_group(branch, dilation, [later(0, per * (i + 1) + u) for u in range(per)])
                return c
            lax.fori_loop(0, (n_blk - per) // per, body, 0)

    rows_per = 256

    def merge(i, carry):
        rows = pl.ds(pl.multiple_of(i * rows_per, rows_per), rows_per)
        l0, l1, l2 = lse_ref[0, rows, :], lse_ref[1, rows, :], lse_ref[2, rows, :]
        mx = jnp.maximum(jnp.maximum(l0, l1), l2)
        w0, w1, w2 = jnp.exp2(l0 - mx), jnp.exp2(l1 - mx), jnp.exp2(l2 - mx)
        num = w0 * ob_ref[0, rows, :] + w1 * ob_ref[1, rows, :] + w2 * ob_ref[2, rows, :]
        o_ref[0, rows, :] = (num / (w0 + w1 + w2)).astype(o_ref.dtype)
        return carry

    lax.fori_loop(0, seq // rows_per, merge, 0)


def _dilated_attn(qkv, slopes):
    bsz, seq, _ = qkv.shape
    n_pairs = A_HEADS * A_HEAD_DIM // LANES
    assert all(w // d == BAND and seq % (d * BAND) == 0 for w, d in DILATED_PATTERNS)

    def col_spec(part):
        return pl.BlockSpec((1, seq, LANES), lambda b, h, part=part: (b, 0, part * n_pairs + h))

    return pl.pallas_call(
        functools.partial(_dilated_kernel, seq=seq),
        grid=(bsz, n_pairs),
        in_specs=[pl.BlockSpec(memory_space=pltpu.SMEM), col_spec(0), col_spec(1), col_spec(2)],
        out_specs=pl.BlockSpec((1, seq, LANES), lambda b, h: (b, 0, h)),
        out_shape=jax.ShapeDtypeStruct((bsz, seq, A_HEADS * A_HEAD_DIM), BF16),
        scratch_shapes=[pltpu.VMEM((3, seq, LANES), F32), pltpu.VMEM((3, seq, LANES), F32),
                        pltpu.VMEM((2, BAND, 3 * BAND), F32)],
        compiler_params=_params("parallel", "parallel"),
        name="dilated_attn",
    )(slopes, qkv, qkv, qkv)


FFN_COLS = 256


def _tail_kernel(o_ref, x_ref, wo_ref, gm_ref, bm_ref, win_ref, cw_ref, cb_ref, wout_ref, gf_ref, bf_ref,
                 y_ref, u_ref, h_ref, *, tm, tiles_per_seq):
    i = pl.program_id(0)
    halo = SUBLANES

    @pl.when(i % tiles_per_seq == 0)
    def _():
        u_ref[0:halo, :] = jnp.zeros((halo, 2 * D_FF), F32)

    mix = jnp.dot(o_ref[...], wo_ref[...], preferred_element_type=F32)
    x = _layer_norm(ALPHA * x_ref[...] + mix, gm_ref[...], bm_ref[...])
    u_ref[halo:halo + tm, :] = jnp.dot(x.astype(BF16), win_ref[...], preferred_element_type=F32)

    for c in range(D_FF // FFN_COLS):
        def conv(col):
            cols = slice(col, col + FFN_COLS)
            acc = cb_ref[:, cols] + cw_ref[2:3, cols] * u_ref[halo:halo + tm, cols]
            acc = acc + cw_ref[1:2, cols] * u_ref[halo - 1:halo - 1 + tm, cols]
            return acc + cw_ref[0:1, cols] * u_ref[halo - 2:halo - 2 + tm, cols]

        gate = conv(c * FFN_COLS)
        val = conv(D_FF + c * FFN_COLS)
        h_ref[:, c * FFN_COLS:(c + 1) * FFN_COLS] = (jax.nn.silu(gate) * val).astype(BF16)

    u_ref[0:halo, :] = u_ref[tm:tm + halo, :]

    f = jnp.dot(h_ref[...], wout_ref[...], preferred_element_type=F32)
    y_ref[...] = _layer_norm(ALPHA * x + f, gf_ref[...], bf_ref[...])


def _layer_tail(o2d, x2d, seq, w_o, g_mix, b_mix, w_in, conv_w, conv_b, w_out, g_ffn, b_ffn):
    t, d = x2d.shape
    k = o2d.shape[1]
    tm = 256
    assert seq % tm == 0 and D_FF % FFN_COLS == 0 and CONV_WIDTH - 1 <= SUBLANES
    return pl.pallas_call(
        functools.partial(_tail_kernel, tm=tm, tiles_per_seq=seq // tm),
        grid=(t // tm,),
        in_specs=[pl.BlockSpec((tm, k), lambda i: (i, 0)), pl.BlockSpec((tm, d), lambda i: (i, 0)),
                  _const_spec((k, d)), _const_spec((1, d)), _const_spec((1, d)),
                  _const_spec((d, 2 * D_FF)), _const_spec((CONV_WIDTH, 2 * D_FF)), _const_spec((1, 2 * D_FF)),
                  _const_spec((D_FF, d)), _const_spec((1, d)), _const_spec((1, d))],
        out_specs=pl.BlockSpec((tm, d), lambda i: (i, 0)),
        out_shape=jax.ShapeDtypeStruct((t, d), F32),
        scratch_shapes=[pltpu.VMEM((tm + SUBLANES, 2 * D_FF), F32), pltpu.VMEM((tm, D_FF), BF16)],
        compiler_params=_params("arbitrary"),
        name="layer_tail",
    )(o2d, x2d, w_o, g_mix, b_mix, w_in, conv_w, conv_b, w_out, g_ffn, b_ffn)


def _rope(t, cos, sin_signed):
    lane = lax.broadcasted_iota(jnp.int32, t.shape, 1)
    half = B_ROPE_DIM // 2
    first = (lane % B_ROPE_DIM) < half
    swapped = jnp.where(first, pltpu.roll(t, LANES - half, 1), pltpu.roll(t, half, 1))
    return t * cos + swapped * sin_signed


def _rope_t(t, cos, sin_signed):
    half = B_ROPE_DIM // 2
    pieces = [t[r * half:(r + 1) * half] for r in range(t.shape[0] // half)]
    swapped = jnp.concatenate([pieces[r ^ 1] for r in range(len(pieces))], axis=0)
    return t * cos + swapped * sin_signed


def _rms(c, g):
    return c * lax.rsqrt(jnp.mean(c * c, axis=-1, keepdims=True) + RMS_EPS) * g


def _mla_proj_kernel(x_ref, wd_ref, kvg_ref, qg_ref, wuk_ref, wuvt_ref, wuqnt_ref, wuqrt_ref,
                     cos_ref, sin_ref, cost_ref, sint_ref, qn_ref, qr_ref, kn_ref, kr_ref, v_ref):
    nt = (((1,), (1,)), ((), ()))
    x = x_ref[...].astype(BF16)
    down = jnp.dot(x, wd_ref[...], preferred_element_type=F32)

    c_kv = _rms(down[:, :KV_RANK], kvg_ref[...]).astype(BF16)
    kn_ref[...] = jnp.dot(c_kv, wuk_ref[...], preferred_element_type=F32).astype(BF16)
    kr_ref[...] = _rope(down[:, KV_RANK:KV_RANK + LANES], cos_ref[...], sin_ref[...]).astype(BF16)
    v_t = lax.dot_general(wuvt_ref[...], c_kv, nt, preferred_element_type=F32)
    for c in range(v_t.shape[1] // MLA_TK):
        v_ref[0, c] = v_t[:, c * MLA_TK:(c + 1) * MLA_TK].astype(BF16)

    scale = (B_NOPE_DIM + B_ROPE_DIM) ** -0.5 * LOG2E
    c_q = _rms(down[:, KV_RANK + LANES:], qg_ref[...]).astype(BF16)
    qn_ref[0] = (lax.dot_general(wuqnt_ref[...], c_q, nt, preferred_element_type=F32) * scale).astype(BF16)
    qr_t = lax.dot_general(wuqrt_ref[...], c_q, nt, preferred_element_type=F32)
    cos_t, sin_t = cost_ref[...], sint_ref[...]
    for c in range(qr_t.shape[0] // LANES):
        rows = slice(c * LANES, (c + 1) * LANES)
        qr_ref[0, rows] = (_rope_t(qr_t[rows], cos_t, sin_t) * scale).astype(BF16)


def _mla_proj(x2d, seq, w_down, kv_g, q_g, w_uk, w_uv_t, w_uqn_t, w_uqr_t, cos, sin, cos_t, sin_t):
    t, d = x2d.shape
    tm = 512
    n_seq_tiles = seq // tm
    bsz = t // seq
    blk = tm // MLA_TK
    hn, hr, hv = B_HEADS * B_NOPE_DIM, B_HEADS * B_ROPE_DIM, B_HEADS * B_V_DIM
    row = lambda n: pl.BlockSpec((tm, n), lambda i: (i, 0))
    tab = pl.BlockSpec((tm, LANES), lambda i: (i % n_seq_tiles, 0))
    tab_t = pl.BlockSpec((LANES, tm), lambda i: (0, i % n_seq_tiles))
    feat_t = lambda n: pl.BlockSpec((1, n, tm), lambda i: (i // n_seq_tiles, 0, i % n_seq_tiles))
    return pl.pallas_call(
        _mla_proj_kernel,
        grid=(t // tm,),
        in_specs=[row(d), _const_spec(w_down.shape), _const_spec((1, KV_RANK)), _const_spec((1, Q_RANK)),
                  _const_spec(w_uk.shape), _const_spec(w_uv_t.shape), _const_spec(w_uqn_t.shape),
                  _const_spec(w_uqr_t.shape), tab, tab, tab_t, tab_t],
        out_specs=[feat_t(hn), feat_t(hr), row(hn), row(LANES),
                   pl.BlockSpec((1, blk, hv, MLA_TK), lambda i: (i // n_seq_tiles, i % n_seq_tiles, 0, 0))],
        out_shape=[jax.ShapeDtypeStruct((bsz, hn, seq), BF16), jax.ShapeDtypeStruct((bsz, hr, seq), BF16),
                   jax.ShapeDtypeStruct((t, hn), BF16), jax.ShapeDtypeStruct((t, LANES), BF16),
                   jax.ShapeDtypeStruct((bsz, seq // MLA_TK, hv, MLA_TK), BF16)],
        compiler_params=_params("parallel"),
        name="mla_proj",
    )(x2d, w_down, kv_g, q_g, w_uk, w_uv_t, w_uqn_t, w_uqr_t, cos, sin, cos_t, sin_t)


MLA_GROUP = 4
MLA_TQ = 256
MLA_TK = 256
MLA_ONES = 16


def _mla_attn_kernel(qn_ref, qr_ref, kn_ref, kr_ref, v_ref, o_ref, qt_ref, s0_ref, s1_ref, m_ref, acc_ref):
    qi = pl.program_id(2)
    tq, tk = MLA_TQ, MLA_TK
    heads = range(MLA_GROUP)
    ones = jnp.ones((MLA_ONES, tk), BF16)

    for h in heads:
        zn = jnp.zeros((B_NOPE_DIM, tq), BF16)
        zr = jnp.zeros((B_ROPE_DIM, tq), BF16)
        qn = qn_ref[0, h * B_NOPE_DIM:(h + 1) * B_NOPE_DIM, :]
        qr = qr_ref[0, h * B_ROPE_DIM:(h + 1) * B_ROPE_DIM, :]
        qt_ref[h] = jnp.concatenate([qn, zn][::1 if h % 2 == 0 else -1]
                                    + [qr if g == h else zr for g in heads], axis=0)
    m_ref[...] = jnp.full(m_ref.shape, NEG, F32)
    acc_ref[...] = jnp.zeros(acc_ref.shape, F32)

    def scores(j, s_ref):
        ks = pl.multiple_of(j * tk, tk)
        kr = kr_ref[0, pl.ds(ks, tk), :]
        for pair in range(MLA_GROUP // 2):
            k_cat = jnp.concatenate([kn_ref[0, pl.ds(ks, tk), pair * LANES:(pair + 1) * LANES], kr], axis=1)
            for h in (2 * pair, 2 * pair + 1):
                s_ref[h] = jnp.dot(k_cat, qt_ref[h], preferred_element_type=F32)

    def absorb(j, s_ref, masked):
        if masked:
            kl = lax.broadcasted_iota(jnp.int32, (tk, tq), 0)
            ql = lax.broadcasted_iota(jnp.int32, (tk, tq), 1)
            causal = kl <= ql
        mp_all = []
        for h in heads:
            s, m_old = s_ref[h], m_ref[h]
            if masked:
                s = jnp.where(causal, s, NEG)
            m_new = jnp.maximum(m_old, jnp.max(s, axis=0, keepdims=True))
            mp_all.append((m_old, m_new, jnp.exp2(s - m_new).astype(BF16)))
        for h in heads:
            m_old, m_new, p = mp_all[h]
            v1 = jnp.concatenate([v_ref[0, j, h * B_V_DIM:(h + 1) * B_V_DIM, :], ones], axis=0)
            pv = jnp.dot(v1, p, preferred_element_type=F32)
            acc_ref[h] = jnp.exp2(m_old - m_new) * acc_ref[h] + pv
            m_ref[h] = m_new

    scores(0, s0_ref)

    def two_blocks(i, carry):
        j = 2 * i
        scores(j + 1, s1_ref)
        absorb(j, s0_ref, False)
        scores(j + 2, s0_ref)
        absorb(j + 1, s1_ref, False)
        return carry

    lax.fori_loop(0, qi // 2, two_blocks, 0)

    @pl.when(qi % 2 == 0)
    def _():
        absorb(qi, s0_ref, True)

    @pl.when(qi % 2 == 1)
    def _():
        scores(qi, s1_ref)
        absorb(qi - 1, s0_ref, False)
        absorb(qi, s1_ref, True)

    o_t = jnp.concatenate([acc_ref[h, :B_V_DIM] / acc_ref[h, B_V_DIM:B_V_DIM + 1] for h in heads], axis=0)
    o_ref[0] = o_t.T.astype(o_ref.dtype)


def _mla_attn(qn_t, qr_t, kn, kr, v_t):
    bsz, _, seq = qn_t.shape
    n_groups = B_HEADS // MLA_GROUP
    n_kblk = seq // MLA_TK
    gn, gr, gv = MLA_GROUP * B_NOPE_DIM, MLA_GROUP * B_ROPE_DIM, MLA_GROUP * B_V_DIM
    assert MLA_TQ == MLA_TK and seq % MLA_TQ == 0 and gr == LANES and gn == 2 * LANES
    score_buf = pltpu.VMEM((MLA_GROUP, MLA_TK, MLA_TQ), F32)
    return pl.pallas_call(
        _mla_attn_kernel,
        grid=(bsz, n_groups, seq // MLA_TQ),
        in_specs=[pl.BlockSpec((1, gn, MLA_TQ), lambda b, g, i: (b, g, i)),
                  pl.BlockSpec((1, gr, MLA_TQ), lambda b, g, i: (b, g, i)),
                  pl.BlockSpec((1, seq, gn), lambda b, g, i: (b, 0, g)),
                  pl.BlockSpec((1, seq, LANES), lambda b, g, i: (b, 0, 0)),
                  pl.BlockSpec((1, n_kblk, gv, MLA_TK), lambda b, g, i: (b, 0, g, 0))],
        out_specs=pl.BlockSpec((1, MLA_TQ, gv), lambda b, g, i: (b, i, g)),
        out_shape=jax.ShapeDtypeStruct((bsz, seq, B_HEADS * B_V_DIM), BF16),
        scratch_shapes=[pltpu.VMEM((MLA_GROUP, 2 * LANES, MLA_TQ), BF16),
                        score_buf, score_buf,
                        pltpu.VMEM((MLA_GROUP, 1, MLA_TQ), F32),
                        pltpu.VMEM((MLA_GROUP, B_V_DIM + MLA_ONES, MLA_TQ), F32)],
        compiler_params=_params("parallel", "parallel", "parallel"),
        name="mla_attn",
    )(qn_t, qr_t, kn, kr, v_t)


def _rope_tables(seq):
    inv_freq = ROPE_BASE ** (-jnp.arange(0, B_ROPE_DIM, 2, dtype=F32) / B_ROPE_DIM)
    ang = jnp.arange(seq, dtype=F32)[:, None] * inv_freq[None, :]
    cos, sin = jnp.cos(ang), jnp.sin(ang)
    reps = LANES // B_ROPE_DIM
    cos = jnp.tile(jnp.concatenate([cos, cos], axis=1), (1, reps))
    sin = jnp.tile(jnp.concatenate([-sin, sin], axis=1), (1, reps))
    return cos, sin, cos.T, sin.T


def kernel(x, a_w_qkv, a_w_o, kv_w_dkv, kv_norm_g, kv_w_kr, kv_w_uk, kv_w_uv, b_w_dq, b_q_norm_g, b_w_uq, b_w_o, ffn_w_in, ffn_conv_w, ffn_conv_b, ffn_w_out, ln_mix_g, ln_mix_b, ln_ffn_g, ln_ffn_b):
    bsz, seq, d = x.shape
    t = bsz * seq
    x2d = x.reshape(t, d)
    row = lambda a: a.reshape(1, -1)

    def tail(o, xin, w_o, layer):
        return _layer_tail(o.reshape(t, -1), xin, seq, w_o.astype(BF16), row(ln_mix_g[layer]), row(ln_mix_b[layer]),
                           ffn_w_in[layer].astype(BF16), ffn_conv_w[layer], row(ffn_conv_b[layer]),
                           ffn_w_out[layer].astype(BF16), row(ln_ffn_g[layer]), row(ln_ffn_b[layer]))

    slopes = jnp.asarray([2.0 ** (-8.0 * (h + 1) / A_HEADS) for h in range(A_HEADS)], dtype=F32)
    qkv = _qkv_proj(x2d, a_w_qkv[0].astype(BF16))
    o = _dilated_attn(qkv.reshape(bsz, seq, -1), slopes)
    x2d = tail(o, x2d, a_w_o[0], 0)

    cos, sin, cos_t, sin_t = _rope_tables(seq)
    w_down = jnp.concatenate([kv_w_dkv, jnp.tile(kv_w_kr, (1, LANES // B_ROPE_DIM)), b_w_dq[0]], axis=1)
    w_uq = b_w_uq[0].reshape(Q_RANK, B_HEADS, B_NOPE_DIM + B_ROPE_DIM)
    w_uqn_t = w_uq[:, :, :B_NOPE_DIM].reshape(Q_RANK, -1).T
    w_uqr_t = w_uq[:, :, B_NOPE_DIM:].reshape(Q_RANK, -1).T
    qn_t, qr_t, kn, kr, v_t = _mla_proj(x2d, seq, w_down.astype(BF16), row(kv_norm_g), row(b_q_norm_g[0]),
                                        kv_w_uk.astype(BF16), kv_w_uv.T.astype(BF16), w_uqn_t.astype(BF16),
                                        w_uqr_t.astype(BF16), cos, sin, cos_t, sin_t)
    shp = lambda a: a.reshape(bsz, seq, -1)
    o = _mla_attn(qn_t, qr_t, shp(kn), shp(kr), v_t)
    x2d = tail(o, x2d, b_w_o[0], 1)
    return x2d.reshape(bsz, seq, d)
```

```python
import functools

import jax
import jax.numpy as jnp
from jax import lax
from jax.experimental import pallas as pl
from jax.experimental.pallas import tpu as pltpu

D_MODEL = 1024
DEPTH = 2
A_HEADS = 16
A_HEAD_DIM = 64
DILATED_PATTERNS = ((128, 1), (512, 4), (2048, 16))
BAND = 128
B_HEADS = 16
B_NOPE_DIM = 64
B_ROPE_DIM = 32
B_V_DIM = 64
Q_RANK = 384
KV_RANK = 256
ROPE_BASE = 10000.0
D_FF = 2816
CONV_WIDTH = 3
ALPHA = (2.0 * DEPTH) ** 0.25
LN_EPS = 1e-5
RMS_EPS = 1e-6

LANES = 128
SUBLANES = 8
NEG = -1e30
LOG2E = 1.4426950408889634
VMEM_LIMIT = 56 * 1024 * 1024

F32 = jnp.float32
BF16 = jnp.bfloat16


def _const_spec(shape):
    nd = len(shape)
    return pl.BlockSpec(shape, lambda *_: (0,) * nd, pipeline_mode=pl.Buffered(1))


def _params(*sem):
    return pltpu.CompilerParams(dimension_semantics=sem, vmem_limit_bytes=VMEM_LIMIT)


def _layer_norm(y, g, b):
    mu = jnp.mean(y, axis=-1, keepdims=True)
    yc = y - mu
    var = jnp.mean(yc * yc, axis=-1, keepdims=True)
    return yc * lax.rsqrt(var + LN_EPS) * g + b


def _qkv_kernel(x_ref, w_ref, o_ref):
    o_ref[...] = jnp.dot(x_ref[...].astype(BF16), w_ref[...], preferred_element_type=F32)


def _qkv_proj(x2d, w):
    t, d = x2d.shape
    n = w.shape[1]
    tm = 512
    return pl.pallas_call(
        _qkv_kernel,
        grid=(t // tm,),
        in_specs=[pl.BlockSpec((tm, d), lambda i: (i, 0)), _const_spec((d, n))],
        out_specs=pl.BlockSpec((tm, n), lambda i: (i, 0)),
        out_shape=jax.ShapeDtypeStruct((t, n), F32),
        compiler_params=_params("parallel"),
        name="qkv_proj",
    )(x2d, w)


DIL_GROUP_FIRST = 8
DIL_GROUP_RESIDUES = 2
DIL_GROUP_LATER = 4


def _dilated_kernel(slopes_ref, q_ref, k_ref, v_ref, o_ref, ob_ref, lse_ref, bias_ref, *, seq):
    hp = pl.program_id(1)
    lane = lax.broadcasted_iota(jnp.int32, (1, LANES), 1)
    head0 = lane < A_HEAD_DIM
    head_masks = (head0, jnp.logical_not(head0))
    q_scale = A_HEAD_DIM ** -0.5 * LOG2E
    ones = jnp.ones((1, LANES), BF16)

    def fill_bias(dilation):
        for off, width in ((0, BAND), (BAND, 2 * BAND)):
            ql = lax.broadcasted_iota(jnp.int32, (BAND, width), 0)
            kl = lax.broadcasted_iota(jnp.int32, (BAND, width), 1)
            back = ql - kl + (width - BAND)
            valid = (back >= 0) & (back <= BAND)
            dist = (back * dilation).astype(F32)
            for hh in range(2):
                slope = slopes_ref[2 * hp + hh] * LOG2E
                bias_ref[hh, :, off:off + width] = jnp.where(valid, -slope * dist, NEG)

    def load_unit(dilation, q_start, k_start, width):
        q = (q_ref[0, pl.ds(q_start, BAND, stride=dilation), :] * q_scale).astype(BF16)
        k = k_ref[0, pl.ds(k_start, width, stride=dilation), :].astype(BF16)
        v = v_ref[0, pl.ds(k_start, width, stride=dilation), :].astype(BF16)
        return q, k, v

    def scores(q, k):
        width = k.shape[0]
        boff = 0 if width == BAND else BAND
        qs = jnp.concatenate([jnp.where(mask, q, jnp.zeros_like(q)) for mask in head_masks], axis=0)
        s = lax.dot_general(qs, k, (((1,), (1,)), ((), ())), preferred_element_type=F32)
        return [s[hh * BAND:(hh + 1) * BAND] + bias_ref[hh, :, boff:boff + width] for hh in range(2)]

    def softmax(s):
        m = jnp.max(s, axis=-1, keepdims=True)
        return m, jnp.exp2(s - m).astype(BF16)

    def weighted(p, v, hh):
        return jnp.dot(p, jnp.where(head_masks[hh], v, ones), preferred_element_type=F32)

    def run_group(branch, dilation, units):
        loaded = [load_unit(dilation, *u) for u in units]
        s_all = [scores(q, k) for q, k, _ in loaded]
        mp_all = [[softmax(s) for s in s_u] for s_u in s_all]
        pv_all = [[weighted(mp[hh][1], v, hh) for hh in range(2)] for mp, (_, _, v) in zip(mp_all, loaded)]
        for (q_start, _, _), mp, pv in zip(units, mp_all, pv_all):
            num = jnp.where(head0, pv[0], pv[1])
            den = pltpu.roll(jnp.where(head0, pv[1], pv[0]), A_HEAD_DIM, 1)
            rows = pl.ds(q_start, BAND, stride=dilation)
            ob_ref[branch, rows, :] = num / den
            lse_ref[branch, rows, :] = jnp.where(head0, mp[0][0], mp[1][0]) + jnp.log2(den)

    for branch, (_, dilation) in enumerate(DILATED_PATTERNS):
        n_blk = seq // dilation // BAND
        blk_stride = BAND * dilation
        fill_bias(dilation)
        first = lambda r: (r, r, BAND)
        later = lambda r, n, blk_stride=blk_stride: (n * blk_stride + r, (n - 1) * blk_stride + r, 2 * BAND)

        if n_blk == 1:
            group = DIL_GROUP_FIRST
            def body(g, c, branch=branch, dilation=dilation, group=group):
                run_group(branch, dilation, [first(g * group + u) for u in range(group)])
                return c
            lax.fori_loop(0, dilation // group, body, 0)
        elif dilation > 1:
            res = DIL_GROUP_RESIDUES
            def body(g, c, branch=branch, dilation=dilation, n_blk=n_blk, res=res):
                units = []
                for u in range(res):
                    r = g * res + u
                    units += [first(r)] + [later(r, n) for n in range(1, n_blk)]
                run_group(branch, dilation, units)
                return c
            lax.fori_loop(0, dilation // res, body, 0)
        else:
            per = DIL_GROUP_LATER
            assert (n_blk - per) % per == 0
            run_group(branch, dilation, [first(0)] + [later(0, n) for n in range(1, per)])

            def body(i, c, branch=branch, dilation=dilation, per=per):
                run_group(branch, dilation, [later(0, per * (i + 1) + u) for u in range(per)])
                return c
            lax.fori_loop(0, (n_blk - per) // per, body, 0)

    rows_per = 256

    def merge(i, carry):
        rows = pl.ds(pl.multiple_of(i * rows_per, rows_per), rows_per)
        l0, l1, l2 = lse_ref[0, rows, :], lse_ref[1, rows, :], lse_ref[2, rows, :]
        mx = jnp.maximum(jnp.maximum(l0, l1), l2)
        w0, w1, w2 = jnp.exp2(l0 - mx), jnp.exp2(l1 - mx), jnp.exp2(l2 - mx)
        num = w0 * ob_ref[0, rows, :] + w1 * ob_ref[1, rows, :] + w2 * ob_ref[2, rows, :]
        o_ref[0, rows, :] = (num / (w0 + w1 + w2)).astype(o_ref.dtype)
        return carry

    lax.fori_loop(0, seq // rows_per, merge, 0)


def _dilated_attn(qkv, slopes):
    bsz, seq, _ = qkv.shape
    n_pairs = A_HEADS * A_HEAD_DIM // LANES
    assert all(w // d == BAND and seq % (d * BAND) == 0 for w, d in DILATED_PATTERNS)

    def col_spec(part):
        return pl.BlockSpec((1, seq, LANES), lambda b, h, part=part: (b, 0, part * n_pairs + h))

    return pl.pallas_call(
        functools.partial(_dilated_kernel, seq=seq),
        grid=(bsz, n_pairs),
        in_specs=[pl.BlockSpec(memory_space=pltpu.SMEM), col_spec(0), col_spec(1), col_spec(2)],
        out_specs=pl.BlockSpec((1, seq, LANES), lambda b, h: (b, 0, h)),
        out_shape=jax.ShapeDtypeStruct((bsz, seq, A_HEADS * A_HEAD_DIM), BF16),
        scratch_shapes=[pltpu.VMEM((3, seq, LANES), F32), pltpu.VMEM((3, seq, LANES), F32),
                        pltpu.VMEM((2, BAND, 3 * BAND), F32)],
        compiler_params=_params("parallel", "parallel"),
        name="dilated_attn",
    )(slopes, qkv, qkv, qkv)


FFN_COLS = 256


def _tail_kernel(o_ref, x_ref, wo_ref, gm_ref, bm_ref, win_ref, cw_ref, cb_ref, wout_ref, gf_ref, bf_ref,
                 y_ref, u_ref, h_ref, *, tm, tiles_per_seq):
    i = pl.program_id(0)
    halo = SUBLANES

    @pl.when(i % tiles_per_seq == 0)
    def _():
        u_ref[0:halo, :] = jnp.zeros((halo, 2 * D_FF), F32)

    mix = jnp.dot(o_ref[...], wo_ref[...], preferred_element_type=F32)
    x = _layer_norm(ALPHA * x_ref[...] + mix, gm_ref[...], bm_ref[...])
    u_ref[halo:halo + tm, :] = jnp.dot(x.astype(BF16), win_ref[...], preferred_element_type=F32)

    for c in range(D_FF // FFN_COLS):
        def conv(col):
            cols = slice(col, col + FFN_COLS)
            acc = cb_ref[:, cols] + cw_ref[2:3, cols] * u_ref[halo:halo + tm, cols]
            acc = acc + cw_ref[1:2, cols] * u_ref[halo - 1:halo - 1 + tm, cols]
            return acc + cw_ref[0:1, cols] * u_ref[halo - 2:halo - 2 + tm, cols]

        gate = conv(c * FFN_COLS)
        val = conv(D_FF + c * FFN_COLS)
        h_ref[:, c * FFN_COLS:(c + 1) * FFN_COLS] = (jax.nn.silu(gate) * val).astype(BF16)

    u_ref[0:halo, :] = u_ref[tm:tm + halo, :]

    f = jnp.dot(h_ref[...], wout_ref[...], preferred_element_type=F32)
    y_ref[...] = _layer_norm(ALPHA * x + f, gf_ref[...], bf_ref[...])


def _layer_tail(o2d, x2d, seq, w_o, g_mix, b_mix, w_in, conv_w, conv_b, w_out, g_ffn, b_ffn):
    t, d = x2d.shape
    k = o2d.shape[1]
    tm = 512
    assert seq % tm == 0 and D_FF % FFN_COLS == 0 and CONV_WIDTH - 1 <= SUBLANES
    return pl.pallas_call(
        functools.partial(_tail_kernel, tm=tm, tiles_per_seq=seq // tm),
        grid=(t // tm,),
        in_specs=[pl.BlockSpec((tm, k), lambda i: (i, 0)), pl.BlockSpec((tm, d), lambda i: (i, 0)),
                  _const_spec((k, d)), _const_spec((1, d)), _const_spec((1, d)),
                  _const_spec((d, 2 * D_FF)), _const_spec((CONV_WIDTH, 2 * D_FF)), _const_spec((1, 2 * D_FF)),
                  _const_spec((D_FF, d)), _const_spec((1, d)), _const_spec((1, d))],
        out_specs=pl.BlockSpec((tm, d), lambda i: (i, 0)),
        out_shape=jax.ShapeDtypeStruct((t, d), F32),
        scratch_shapes=[pltpu.VMEM((tm + SUBLANES, 2 * D_FF), F32), pltpu.VMEM((tm, D_FF), BF16)],
        compiler_params=_params("arbitrary"),
        name="layer_tail",
    )(o2d, x2d, w_o, g_mix, b_mix, w_in, conv_w, conv_b, w_out, g_ffn, b_ffn)


def _rope(t, cos, sin_signed):
    lane = lax.broadcasted_iota(jnp.int32, t.shape, 1)
    half = B_ROPE_DIM // 2
    first = (lane % B_ROPE_DIM) < half
    swapped = jnp.where(first, pltpu.roll(t, LANES - half, 1), pltpu.roll(t, half, 1))
    return t * cos + swapped * sin_signed


def _rope_t(t, cos, sin_signed):
    half = B_ROPE_DIM // 2
    pieces = [t[r * half:(r + 1) * half] for r in range(t.shape[0] // half)]
    swapped = jnp.concatenate([pieces[r ^ 1] for r in range(len(pieces))], axis=0)
    return t * cos + swapped * sin_signed


def _rms(c, g):
    return c * lax.rsqrt(jnp.mean(c * c, axis=-1, keepdims=True) + RMS_EPS) * g


def _mla_proj_kernel(x_ref, wd_ref, kvg_ref, qg_ref, wuk_ref, wuvt_ref, wuqnt_ref, wuqrt_ref,
                     cos_ref, sin_ref, cost_ref, sint_ref, qn_ref, qr_ref, kn_ref, kr_ref, v_ref):
    nt = (((1,), (1,)), ((), ()))
    x = x_ref[...].astype(BF16)
    down = jnp.dot(x, wd_ref[...], preferred_element_type=F32)

    c_kv = _rms(down[:, :KV_RANK], kvg_ref[...]).astype(BF16)
    kn_ref[...] = jnp.dot(c_kv, wuk_ref[...], preferred_element_type=F32).astype(BF16)
    kr_ref[...] = _rope(down[:, KV_RANK:KV_RANK + LANES], cos_ref[...], sin_ref[...]).astype(BF16)
    v_t = lax.dot_general(wuvt_ref[...], c_kv, nt, preferred_element_type=F32)
    for c in range(v_t.shape[1] // MLA_TK):
        v_ref[0, c] = v_t[:, c * MLA_TK:(c + 1) * MLA_TK].astype(BF16)

    scale = (B_NOPE_DIM + B_ROPE_DIM) ** -0.5 * LOG2E
    c_q = _rms(down[:, KV_RANK + LANES:], qg_ref[...]).astype(BF16)
    qn_t = (lax.dot_general(wuqnt_ref[...], c_q, nt, preferred_element_type=F32) * scale).astype(BF16)
    qr_t = lax.dot_general(wuqrt_ref[...], c_q, nt, preferred_element_type=F32)
    cos_t, sin_t = cost_ref[...], sint_ref[...]
    qr_t = jnp.concatenate([(_rope_t(qr_t[c * LANES:(c + 1) * LANES], cos_t, sin_t) * scale).astype(BF16)
                            for c in range(qr_t.shape[0] // LANES)], axis=0)
    for c in range(qn_t.shape[1] // MLA_TQ):
        cols = slice(c * MLA_TQ, (c + 1) * MLA_TQ)
        qn_ref[0, c] = qn_t[:, cols]
        qr_ref[0, c] = qr_t[:, cols]


def _mla_proj(x2d, seq, w_down, kv_g, q_g, w_uk, w_uv_t, w_uqn_t, w_uqr_t, cos, sin, cos_t, sin_t):
    t, d = x2d.shape
    tm = 512
    n_seq_tiles = seq // tm
    bsz = t // seq
    blk = tm // MLA_TK
    hn, hr, hv = B_HEADS * B_NOPE_DIM, B_HEADS * B_ROPE_DIM, B_HEADS * B_V_DIM
    row = lambda n: pl.BlockSpec((tm, n), lambda i: (i, 0))
    tab = pl.BlockSpec((tm, LANES), lambda i: (i % n_seq_tiles, 0))
    tab_t = pl.BlockSpec((LANES, tm), lambda i: (0, i % n_seq_tiles))
    feat_t = lambda n: pl.BlockSpec((1, blk, n, MLA_TK), lambda i: (i // n_seq_tiles, i % n_seq_tiles, 0, 0))
    feat_shape = lambda n: jax.ShapeDtypeStruct((bsz, seq // MLA_TK, n, MLA_TK), BF16)
    return pl.pallas_call(
        _mla_proj_kernel,
        grid=(t // tm,),
        in_specs=[row(d), _const_spec(w_down.shape), _const_spec((1, KV_RANK)), _const_spec((1, Q_RANK)),
                  _const_spec(w_uk.shape), _const_spec(w_uv_t.shape), _const_spec(w_uqn_t.shape),
                  _const_spec(w_uqr_t.shape), tab, tab, tab_t, tab_t],
        out_specs=[feat_t(hn), feat_t(hr), row(hn), row(LANES), feat_t(hv)],
        out_shape=[feat_shape(hn), feat_shape(hr),
                   jax.ShapeDtypeStruct((t, hn), BF16), jax.ShapeDtypeStruct((t, LANES), BF16), feat_shape(hv)],
        compiler_params=_params("parallel"),
        name="mla_proj",
    )(x2d, w_down, kv_g, q_g, w_uk, w_uv_t, w_uqn_t, w_uqr_t, cos, sin, cos_t, sin_t)


MLA_GROUP = 4
MLA_TQ = 256
MLA_TK = 256
MLA_ONES = 16


def _mla_attn_kernel(qn_ref, qr_ref, kn_ref, kr_ref, v_ref, o_ref, qt_ref, s0_ref, s1_ref, m_ref, acc_ref,
                     *, n_qblk):
    tq, tk = MLA_TQ, MLA_TK
    heads = range(MLA_GROUP)
    ones = jnp.ones((MLA_ONES, tk), BF16)

    def scores(j, s_ref):
        ks = pl.multiple_of(j * tk, tk)
        kr = kr_ref[0, pl.ds(ks, tk), :]
        for pair in range(MLA_GROUP // 2):
            k_cat = jnp.concatenate([kn_ref[0, pl.ds(ks, tk), pair * LANES:(pair + 1) * LANES], kr], axis=1)
            for h in (2 * pair, 2 * pair + 1):
                s_ref[h] = jnp.dot(k_cat, qt_ref[h], preferred_element_type=F32)

    def absorb(j, s_ref, masked):
        if masked:
            kl = lax.broadcasted_iota(jnp.int32, (tk, tq), 0)
            ql = lax.broadcasted_iota(jnp.int32, (tk, tq), 1)
            causal = kl <= ql
        mp_all = []
        for h in heads:
            s, m_old = s_ref[h], m_ref[h]
            if masked:
                s = jnp.where(causal, s, NEG)
            m_new = jnp.maximum(m_old, jnp.max(s, axis=0, keepdims=True))
            mp_all.append((m_old, m_new, jnp.exp2(s - m_new).astype(BF16)))
        for h in heads:
            m_old, m_new, p = mp_all[h]
            v1 = jnp.concatenate([v_ref[0, j, h * B_V_DIM:(h + 1) * B_V_DIM, :], ones], axis=0)
            pv = jnp.dot(v1, p, preferred_element_type=F32)
            acc_ref[h] = jnp.exp2(m_old - m_new) * acc_ref[h] + pv
            m_ref[h] = m_new

    def two_blocks(i, carry):
        j = 2 * i
        scores(j + 1, s1_ref)
        absorb(j, s0_ref, False)
        scores(j + 2, s0_ref)
        absorb(j + 1, s1_ref, False)
        return carry

    def query_block(qi, carry):
        for h in heads:
            zn = jnp.zeros((B_NOPE_DIM, tq), BF16)
            zr = jnp.zeros((B_ROPE_DIM, tq), BF16)
            qn = qn_ref[0, qi, h * B_NOPE_DIM:(h + 1) * B_NOPE_DIM, :]
            qr = qr_ref[0, qi, h * B_ROPE_DIM:(h + 1) * B_ROPE_DIM, :]
            qt_ref[h] = jnp.concatenate([qn, zn][::1 if h % 2 == 0 else -1]
                                        + [qr if g == h else zr for g in heads], axis=0)
        m_ref[...] = jnp.full(m_ref.shape, NEG, F32)
        acc_ref[...] = jnp.zeros(acc_ref.shape, F32)

        scores(0, s0_ref)
        lax.fori_loop(0, qi // 2, two_blocks, 0)

        @pl.when(qi % 2 == 0)
        def _():
            absorb(qi, s0_ref, True)

        @pl.when(qi % 2 == 1)
        def _():
            scores(qi, s1_ref)
            absorb(qi - 1, s0_ref, False)
            absorb(qi, s1_ref, True)

        o_t = jnp.concatenate([acc_ref[h, :B_V_DIM] / acc_ref[h, B_V_DIM:B_V_DIM + 1] for h in heads], axis=0)
        o_ref[0, pl.ds(pl.multiple_of(qi * tq, tq), tq), :] = o_t.T.astype(o_ref.dtype)
        return carry

    lax.fori_loop(0, n_qblk, query_block, 0)


def _mla_attn(qn_t, qr_t, kn, kr, v_t):
    bsz, n_blk, _, _ = qn_t.shape
    seq = n_blk * MLA_TQ
    n_groups = B_HEADS // MLA_GROUP
    gn, gr, gv = MLA_GROUP * B_NOPE_DIM, MLA_GROUP * B_ROPE_DIM, MLA_GROUP * B_V_DIM
    assert MLA_TQ == MLA_TK and gr == LANES and gn == 2 * LANES
    score_buf = pltpu.VMEM((MLA_GROUP, MLA_TK, MLA_TQ), F32)
    blocked = lambda n: pl.BlockSpec((1, n_blk, n, MLA_TK), lambda b, g: (b, 0, g, 0))
    return pl.pallas_call(
        functools.partial(_mla_attn_kernel, n_qblk=n_blk),
        grid=(bsz, n_groups),
        in_specs=[blocked(gn), blocked(gr),
                  pl.BlockSpec((1, seq, gn), lambda b, g: (b, 0, g)),
                  pl.BlockSpec((1, seq, LANES), lambda b, g: (b, 0, 0)),
                  blocked(gv)],
        out_specs=pl.BlockSpec((1, seq, gv), lambda b, g: (b, 0, g)),
        out_shape=jax.ShapeDtypeStruct((bsz, seq, B_HEADS * B_V_DIM), BF16),
        scratch_shapes=[pltpu.VMEM((MLA_GROUP, 2 * LANES, MLA_TQ), BF16),
                        score_buf, score_buf,
                        pltpu.VMEM((MLA_GROUP, 1, MLA_TQ), F32),
                        pltpu.VMEM((MLA_GROUP, B_V_DIM + MLA_ONES, MLA_TQ), F32)],
        compiler_params=_params("parallel", "parallel"),
        name="mla_attn",
    )(qn_t, qr_t, kn, kr, v_t)


def _rope_tables(seq):
    inv_freq = ROPE_BASE ** (-jnp.arange(0, B_ROPE_DIM, 2, dtype=F32) / B_ROPE_DIM)
    ang = jnp.arange(seq, dtype=F32)[:, None] * inv_freq[None, :]
    cos, sin = jnp.cos(ang), jnp.sin(ang)
    reps = LANES // B_ROPE_DIM
    cos = jnp.tile(jnp.concatenate([cos, cos], axis=1), (1, reps))
    sin = jnp.tile(jnp.concatenate([-sin, sin], axis=1), (1, reps))
    return cos, sin, cos.T, sin.T


def kernel(x, a_w_qkv, a_w_o, kv_w_dkv, kv_norm_g, kv_w_kr, kv_w_uk, kv_w_uv, b_w_dq, b_q_norm_g, b_w_uq, b_w_o, ffn_w_in, ffn_conv_w, ffn_conv_b, ffn_w_out, ln_mix_g, ln_mix_b, ln_ffn_g, ln_ffn_b):
    bsz, seq, d = x.shape
    t = bsz * seq
    x2d = x.reshape(t, d)
    row = lambda a: a.reshape(1, -1)

    def tail(o, xin, w_o, layer):
        return _layer_tail(o.reshape(t, -1), xin, seq, w_o.astype(BF16), row(ln_mix_g[layer]), row(ln_mix_b[layer]),
                           ffn_w_in[layer].astype(BF16), ffn_conv_w[layer], row(ffn_conv_b[layer]),
                           ffn_w_out[layer].astype(BF16), row(ln_ffn_g[layer]), row(ln_ffn_b[layer]))

    slopes = jnp.asarray([2.0 ** (-8.0 * (h + 1) / A_HEADS) for h in range(A_HEADS)], dtype=F32)
    qkv = _qkv_proj(x2d, a_w_qkv[0].astype(BF16))
    o = _dilated_attn(qkv.reshape(bsz, seq, -1), slopes)
    x2d = tail(o, x2d, a_w_o[0], 0)

    cos, sin, cos_t, sin_t = _rope_tables(seq)
    w_down = jnp.concatenate([kv_w_dkv, jnp.tile(kv_w_kr, (1, LANES // B_ROPE_DIM)), b_w_dq[0]], axis=1)
    w_uq = b_w_uq[0].reshape(Q_RANK, B_HEADS, B_NOPE_DIM + B_ROPE_DIM)
    w_uqn_t = w_uq[:, :, :B_NOPE_DIM].reshape(Q_RANK, -1).T
    w_uqr_t = w_uq[:, :, B_NOPE_DIM:].reshape(Q_RANK, -1).T
    qn_t, qr_t, kn, kr, v_t = _mla_proj(x2d, seq, w_down.astype(BF16), row(kv_norm_g), row(b_q_norm_g[0]),
                                        kv_w_uk.astype(BF16), kv_w_uv.T.astype(BF16), w_uqn_t.astype(BF16),
                                        w_uqr_t.astype(BF16), cos, sin, cos_t, sin_t)
    shp = lambda a: a.reshape(bsz, seq, -1)
    o = _mla_attn(qn_t, qr_t, shp(kn), shp(kr), v_t)
    x2d = tail(o, x2d, b_w_o[0], 1)
    return x2d.reshape(bsz, seq, d)
```

```python
import functools

import jax
import jax.numpy as jnp
from jax import lax
from jax.experimental import pallas as pl
from jax.experimental.pallas import tpu as pltpu

D_MODEL = 1024
DEPTH = 2
A_HEADS = 16
A_HEAD_DIM = 64
DILATED_PATTERNS = ((128, 1), (512, 4), (2048, 16))
BAND = 128
B_HEADS = 16
B_NOPE_DIM = 64
B_ROPE_DIM = 32
B_V_DIM = 64
Q_RANK = 384
KV_RANK = 256
ROPE_BASE = 10000.0
D_FF = 2816
CONV_WIDTH = 3
ALPHA = (2.0 * DEPTH) ** 0.25
LN_EPS = 1e-5
RMS_EPS = 1e-6

LANES = 128
SUBLANES = 8
NEG = -1e30
LOG2E = 1.4426950408889634
VMEM_LIMIT = 56 * 1024 * 1024

F32 = jnp.float32
BF16 = jnp.bfloat16


def _const_spec(shape):
    nd = len(shape)
    return pl.BlockSpec(shape, lambda *_: (0,) * nd, pipeline_mode=pl.Buffered(1))


def _params(*sem):
    return pltpu.CompilerParams(dimension_semantics=sem, vmem_limit_bytes=VMEM_LIMIT)


def _layer_norm(y, g, b):
    mu = jnp.mean(y, axis=-1, keepdims=True)
    yc = y - mu
    var = jnp.mean(yc * yc, axis=-1, keepdims=True)
    return yc * lax.rsqrt(var + LN_EPS) * g + b


def _qkv_kernel(x_ref, w_ref, o_ref):
    o_ref[...] = jnp.dot(x_ref[...].astype(BF16), w_ref[...], preferred_element_type=F32)


def _qkv_proj(x2d, w):
    t, d = x2d.shape
    n = w.shape[1]
    tm = 512
    return pl.pallas_call(
        _qkv_kernel,
        grid=(t // tm,),
        in_specs=[pl.BlockSpec((tm, d), lambda i: (i, 0)), _const_spec((d, n))],
        out_specs=pl.BlockSpec((tm, n), lambda i: (i, 0)),
        out_shape=jax.ShapeDtypeStruct((t, n), F32),
        compiler_params=_params("parallel"),
        name="qkv_proj",
    )(x2d, w)


DIL_GROUP_FIRST = 8
DIL_GROUP_RESIDUES = 2
DIL_GROUP_LATER = 8


def _dilated_kernel(slopes_ref, q_ref, k_ref, v_ref, o_ref, ob_ref, lse_ref, bias_ref, *, seq):
    hp = pl.program_id(1)
    lane = lax.broadcasted_iota(jnp.int32, (1, LANES), 1)
    head0 = lane < A_HEAD_DIM
    head_masks = (head0, jnp.logical_not(head0))
    q_scale = A_HEAD_DIM ** -0.5 * LOG2E
    ones = jnp.ones((1, LANES), BF16)

    def fill_bias(dilation):
        for off, width in ((0, BAND), (BAND, 2 * BAND)):
            ql = lax.broadcasted_iota(jnp.int32, (BAND, width), 0)
            kl = lax.broadcasted_iota(jnp.int32, (BAND, width), 1)
            back = ql - kl + (width - BAND)
            valid = (back >= 0) & (back <= BAND)
            dist = (back * dilation).astype(F32)
            for hh in range(2):
                slope = slopes_ref[2 * hp + hh] * LOG2E
                bias_ref[hh, :, off:off + width] = jnp.where(valid, -slope * dist, NEG)

    def load_unit(dilation, q_start, k_start, width):
        q = (q_ref[0, pl.ds(q_start, BAND, stride=dilation), :] * q_scale).astype(BF16)
        k = k_ref[0, pl.ds(k_start, width, stride=dilation), :].astype(BF16)
        v = v_ref[0, pl.ds(k_start, width, stride=dilation), :].astype(BF16)
        return q, k, v

    def scores(q, k):
        width = k.shape[0]
        boff = 0 if width == BAND else BAND
        qs = jnp.concatenate([jnp.where(mask, q, jnp.zeros_like(q)) for mask in head_masks], axis=0)
        s = lax.dot_general(qs, k, (((1,), (1,)), ((), ())), preferred_element_type=F32)
        return [s[hh * BAND:(hh + 1) * BAND] + bias_ref[hh, :, boff:boff + width] for hh in range(2)]

    def softmax(s):
        m = jnp.max(s, axis=-1, keepdims=True)
        return m, jnp.exp2(s - m).astype(BF16)

    def weighted(p, v, hh):
        return jnp.dot(p, jnp.where(head_masks[hh], v, ones), preferred_element_type=F32)

    def run_group(branch, dilation, units):
        loaded = [load_unit(dilation, *u) for u in units]
        s_all = [scores(q, k) for q, k, _ in loaded]
        mp_all = [[softmax(s) for s in s_u] for s_u in s_all]
        pv_all = [[weighted(mp[hh][1], v, hh) for hh in range(2)] for mp, (_, _, v) in zip(mp_all, loaded)]
        for (q_start, _, _), mp, pv in zip(units, mp_all, pv_all):
            num = jnp.where(head0, pv[0], pv[1])
            den = pltpu.roll(jnp.where(head0, pv[1], pv[0]), A_HEAD_DIM, 1)
            rows = pl.ds(q_start, BAND, stride=dilation)
            ob_ref[branch, rows, :] = num / den
            lse_ref[branch, rows, :] = jnp.where(head0, mp[0][0], mp[1][0]) + jnp.log2(den)

    for branch, (_, dilation) in enumerate(DILATED_PATTERNS):
        n_blk = seq // dilation // BAND
        blk_stride = BAND * dilation
        fill_bias(dilation)
        first = lambda r: (r, r, BAND)
        later = lambda r, n, blk_stride=blk_stride: (n * blk_stride + r, (n - 1) * blk_stride + r, 2 * BAND)

        if n_blk == 1:
            group = DIL_GROUP_FIRST
            def body(g, c, branch=branch, dilation=dilation, group=group):
                run_group(branch, dilation, [first(g * group + u) for u in range(group)])
                return c
            lax.fori_loop(0, dilation // group, body, 0)
        elif dilation > 1:
            res = DIL_GROUP_RESIDUES
            def body(g, c, branch=branch, dilation=dilation, n_blk=n_blk, res=res):
                units = []
                for u in range(res):
                    r = g * res + u
                    units += [first(r)] + [later(r, n) for n in range(1, n_blk)]
                run_group(branch, dilation, units)
                return c
            lax.fori_loop(0, dilation // res, body, 0)
        else:
            per = DIL_GROUP_LATER
            assert (n_blk - per) % per == 0
            run_group(branch, dilation, [first(0)] + [later(0, n) for n in range(1, per)])

            def body(i, c, branch=branch, dilation=dilation, per=per):
                run_group(branch, dilation, [later(0, per * (i + 1) + u) for u in range(per)])
                return c
            lax.fori_loop(0, (n_blk - per) // per, body, 0)

    rows_per = 256

    def merge(i, carry):
        rows = pl.ds(pl.multiple_of(i * rows_per, rows_per), rows_per)
        l0, l1, l2 = lse_ref[0, rows, :], lse_ref[1, rows, :], lse_ref[2, rows, :]
        mx = jnp.maximum(jnp.maximum(l0, l1), l2)
        w0, w1, w2 = jnp.exp2(l0 - mx), jnp.exp2(l1 - mx), jnp.exp2(l2 - mx)
        num = w0 * ob_ref[0, rows, :] + w1 * ob_ref[1, rows, :] + w2 * ob_ref[2, rows, :]
        o_ref[0, rows, :] = (num / (w0 + w1 + w2)).astype(o_ref.dtype)
        return carry

    lax.fori_loop(0, seq // rows_per, merge, 0)


def _dilated_attn(qkv, slopes):
    bsz, seq, _ = qkv.shape
    n_pairs = A_HEADS * A_HEAD_DIM // LANES
    assert all(w // d == BAND and seq % (d * BAND) == 0 for w, d in DILATED_PATTERNS)

    def col_spec(part):
        return pl.BlockSpec((1, seq, LANES), lambda b, h, part=part: (b, 0, part * n_pairs + h))

    return pl.pallas_call(
        functools.partial(_dilated_kernel, seq=seq),
        grid=(bsz, n_pairs),
        in_specs=[pl.BlockSpec(memory_space=pltpu.SMEM), col_spec(0), col_spec(1), col_spec(2)],
        out_specs=pl.BlockSpec((1, seq, LANES), lambda b, h: (b, 0, h)),
        out_shape=jax.ShapeDtypeStruct((bsz, seq, A_HEADS * A_HEAD_DIM), BF16),
        scratch_shapes=[pltpu.VMEM((3, seq, LANES), F32), pltpu.VMEM((3, seq, LANES), F32),
                        pltpu.VMEM((2, BAND, 3 * BAND), F32)],
        compiler_params=_params("parallel", "parallel"),
        name="dilated_attn",
    )(slopes, qkv, qkv, qkv)


FFN_COLS = 256


def _tail_kernel(o_ref, x_ref, wo_ref, gm_ref, bm_ref, win_ref, cw_ref, cb_ref, wout_ref, gf_ref, bf_ref,
                 y_ref, u_ref, h_ref, *, tm, tiles_per_seq):
    i = pl.program_id(0)
    halo = SUBLANES

    @pl.when(i % tiles_per_seq == 0)
    def _():
        u_ref[0:halo, :] = jnp.zeros((halo, 2 * D_FF), F32)

    mix = jnp.dot(o_ref[...], wo_ref[...], preferred_element_type=F32)
    x = _layer_norm(ALPHA * x_ref[...] + mix, gm_ref[...], bm_ref[...])
    u_ref[halo:halo + tm, :] = jnp.dot(x.astype(BF16), win_ref[...], preferred_element_type=F32)

    for c in range(D_FF // FFN_COLS):
        def conv(col):
            cols = slice(col, col + FFN_COLS)
            acc = cb_ref[:, cols] + cw_ref[2:3, cols] * u_ref[halo:halo + tm, cols]
            acc = acc + cw_ref[1:2, cols] * u_ref[halo - 1:halo - 1 + tm, cols]
            return acc + cw_ref[0:1, cols] * u_ref[halo - 2:halo - 2 + tm, cols]

        gate = conv(c * FFN_COLS)
        val = conv(D_FF + c * FFN_COLS)
        h_ref[:, c * FFN_COLS:(c + 1) * FFN_COLS] = (jax.nn.silu(gate) * val).astype(BF16)

    u_ref[0:halo, :] = u_ref[tm:tm + halo, :]

    f = jnp.dot(h_ref[...], wout_ref[...], preferred_element_type=F32)
    y_ref[...] = _layer_norm(ALPHA * x + f, gf_ref[...], bf_ref[...])


def _layer_tail(o2d, x2d, seq, w_o, g_mix, b_mix, w_in, conv_w, conv_b, w_out, g_ffn, b_ffn):
    t, d = x2d.shape
    k = o2d.shape[1]
    tm = 512
    assert seq % tm == 0 and D_FF % FFN_COLS == 0 and CONV_WIDTH - 1 <= SUBLANES
    return pl.pallas_call(
        functools.partial(_tail_kernel, tm=tm, tiles_per_seq=seq // tm),
        grid=(t // tm,),
        in_specs=[pl.BlockSpec((tm, k), lambda i: (i, 0)), pl.BlockSpec((tm, d), lambda i: (i, 0)),
                  _const_spec((k, d)), _const_spec((1, d)), _const_spec((1, d)),
                  _const_spec((d, 2 * D_FF)), _const_spec((CONV_WIDTH, 2 * D_FF)), _const_spec((1, 2 * D_FF)),
                  _const_spec((D_FF, d)), _const_spec((1, d)), _const_spec((1, d))],
        out_specs=pl.BlockSpec((tm, d), lambda i: (i, 0)),
        out_shape=jax.ShapeDtypeStruct((t, d), F32),
        scratch_shapes=[pltpu.VMEM((tm + SUBLANES, 2 * D_FF), F32), pltpu.VMEM((tm, D_FF), BF16)],
        compiler_params=_params("arbitrary"),
        name="layer_tail",
    )(o2d, x2d, w_o, g_mix, b_mix, w_in, conv_w, conv_b, w_out, g_ffn, b_ffn)


def _rope(t, cos, sin_signed):
    lane = lax.broadcasted_iota(jnp.int32, t.shape, 1)
    half = B_ROPE_DIM // 2
    first = (lane % B_ROPE_DIM) < half
    swapped = jnp.where(first, pltpu.roll(t, LANES - half, 1), pltpu.roll(t, half, 1))
    return t * cos + swapped * sin_signed


def _rope_t(t, cos, sin_signed):
    half = B_ROPE_DIM // 2
    pieces = [t[r * half:(r + 1) * half] for r in range(t.shape[0] // half)]
    swapped = jnp.concatenate([pieces[r ^ 1] for r in range(len(pieces))], axis=0)
    return t * cos + swapped * sin_signed


def _rms(c, g):
    return c * lax.rsqrt(jnp.mean(c * c, axis=-1, keepdims=True) + RMS_EPS) * g


def _mla_proj_kernel(x_ref, wd_ref, kvg_ref, qg_ref, wuk_ref, wuvt_ref, wuqnt_ref, wuqrt_ref,
                     cos_ref, sin_ref, cost_ref, sint_ref, qn_ref, qr_ref, kn_ref, kr_ref, v_ref):
    nt = (((1,), (1,)), ((), ()))
    x = x_ref[...].astype(BF16)
    down = jnp.dot(x, wd_ref[...], preferred_element_type=F32)

    c_kv = _rms(down[:, :KV_RANK], kvg_ref[...]).astype(BF16)
    kn_ref[...] = jnp.dot(c_kv, wuk_ref[...], preferred_element_type=F32).astype(BF16)
    kr_ref[...] = _rope(down[:, KV_RANK:KV_RANK + LANES], cos_ref[...], sin_ref[...]).astype(BF16)
    v_t = lax.dot_general(wuvt_ref[...], c_kv, nt, preferred_element_type=F32)
    for c in range(v_t.shape[1] // MLA_TK):
        v_ref[0, c] = v_t[:, c * MLA_TK:(c + 1) * MLA_TK].astype(BF16)

    scale = (B_NOPE_DIM + B_ROPE_DIM) ** -0.5 * LOG2E
    c_q = _rms(down[:, KV_RANK + LANES:], qg_ref[...]).astype(BF16)
    qn_t = (lax.dot_general(wuqnt_ref[...], c_q, nt, preferred_element_type=F32) * scale).astype(BF16)
    qr_t = lax.dot_general(wuqrt_ref[...], c_q, nt, preferred_element_type=F32)
    cos_t, sin_t = cost_ref[...], sint_ref[...]
    qr_t = jnp.concatenate([(_rope_t(qr_t[c * LANES:(c + 1) * LANES], cos_t, sin_t) * scale).astype(BF16)
                            for c in range(qr_t.shape[0] // LANES)], axis=0)
    for c in range(qn_t.shape[1] // MLA_TQ):
        cols = slice(c * MLA_TQ, (c + 1) * MLA_TQ)
        qn_ref[0, c] = qn_t[:, cols]
        qr_ref[0, c] = qr_t[:, cols]


def _mla_proj(x2d, seq, w_down, kv_g, q_g, w_uk, w_uv_t, w_uqn_t, w_uqr_t, cos, sin, cos_t, sin_t):
    t, d = x2d.shape
    tm = 512
    n_seq_tiles = seq // tm
    bsz = t // seq
    blk = tm // MLA_TK
    hn, hr, hv = B_HEADS * B_NOPE_DIM, B_HEADS * B_ROPE_DIM, B_HEADS * B_V_DIM
    row = lambda n: pl.BlockSpec((tm, n), lambda i: (i, 0))
    tab = pl.BlockSpec((tm, LANES), lambda i: (i % n_seq_tiles, 0))
    tab_t = pl.BlockSpec((LANES, tm), lambda i: (0, i % n_seq_tiles))
    feat_t = lambda n: pl.BlockSpec((1, blk, n, MLA_TK), lambda i: (i // n_seq_tiles, i % n_seq_tiles, 0, 0))
    feat_shape = lambda n: jax.ShapeDtypeStruct((bsz, seq // MLA_TK, n, MLA_TK), BF16)
    return pl.pallas_call(
        _mla_proj_kernel,
        grid=(t // tm,),
        in_specs=[row(d), _const_spec(w_down.shape), _const_spec((1, KV_RANK)), _const_spec((1, Q_RANK)),
                  _const_spec(w_uk.shape), _const_spec(w_uv_t.shape), _const_spec(w_uqn_t.shape),
                  _const_spec(w_uqr_t.shape), tab, tab, tab_t, tab_t],
        out_specs=[feat_t(hn), feat_t(hr), row(hn), row(LANES), feat_t(hv)],
        out_shape=[feat_shape(hn), feat_shape(hr),
                   jax.ShapeDtypeStruct((t, hn), BF16), jax.ShapeDtypeStruct((t, LANES), BF16), feat_shape(hv)],
        compiler_params=_params("parallel"),
        name="mla_proj",
    )(x2d, w_down, kv_g, q_g, w_uk, w_uv_t, w_uqn_t, w_uqr_t, cos, sin, cos_t, sin_t)


MLA_GROUP = 8
MLA_TQ = 256
MLA_TK = 256
MLA_ONES = 16


def _mla_attn_kernel(qn_ref, qr_ref, kn_ref, kr_ref, v_ref, o_ref, qt_ref, s0_ref, s1_ref, m_ref, acc_ref,
                     *, n_qblk):
    tq, tk = MLA_TQ, MLA_TK
    heads = range(MLA_GROUP)
    ones = jnp.ones((MLA_ONES, tk), BF16)

    def scores(j, s_ref):
        ks = pl.multiple_of(j * tk, tk)
        kr = kr_ref[0, pl.ds(ks, tk), :]
        for pair in range(MLA_GROUP // 2):
            k_cat = jnp.concatenate([kn_ref[0, pl.ds(ks, tk), pair * LANES:(pair + 1) * LANES], kr], axis=1)
            for h in (2 * pair, 2 * pair + 1):
                s_ref[h] = jnp.dot(k_cat, qt_ref[h], preferred_element_type=F32)

    def absorb(j, s_ref, masked):
        if masked:
            kl = lax.broadcasted_iota(jnp.int32, (tk, tq), 0)
            ql = lax.broadcasted_iota(jnp.int32, (tk, tq), 1)
            causal = kl <= ql
        mp_all = []
        for h in heads:
            s, m_old = s_ref[h], m_ref[h]
            if masked:
                s = jnp.where(causal, s, NEG)
            m_new = jnp.maximum(m_old, jnp.max(s, axis=0, keepdims=True))
            mp_all.append((m_old, m_new, jnp.exp2(s - m_new).astype(BF16)))
        for h in heads:
            m_old, m_new, p = mp_all[h]
            v1 = jnp.concatenate([v_ref[0, j, h * B_V_DIM:(h + 1) * B_V_DIM, :], ones], axis=0)
            pv = jnp.dot(v1, p, preferred_element_type=F32)
            acc_ref[h] = jnp.exp2(m_old - m_new) * acc_ref[h] + pv
            m_ref[h] = m_new

    def two_blocks(i, carry):
        j = 2 * i
        scores(j + 1, s1_ref)
        absorb(j, s0_ref, False)
        scores(j + 2, s0_ref)
        absorb(j + 1, s1_ref, False)
        return carry

    def query_block(qi, carry):
        for h in heads:
            zn = jnp.zeros((B_NOPE_DIM, tq), BF16)
            zr = jnp.zeros((B_ROPE_DIM, tq), BF16)
            qn = qn_ref[0, qi, h * B_NOPE_DIM:(h + 1) * B_NOPE_DIM, :]
            qr = qr_ref[0, qi, h * B_ROPE_DIM:(h + 1) * B_ROPE_DIM, :]
            rope_slots = LANES // B_ROPE_DIM
            qt_ref[h] = jnp.concatenate([qn, zn][::1 if h % 2 == 0 else -1]
                                        + [qr if g == h % rope_slots else zr for g in range(rope_slots)], axis=0)
        m_ref[...] = jnp.full(m_ref.shape, NEG, F32)
        acc_ref[...] = jnp.zeros(acc_ref.shape, F32)

        scores(0, s0_ref)
        lax.fori_loop(0, qi // 2, two_blocks, 0)

        @pl.when(qi % 2 == 0)
        def _():
            absorb(qi, s0_ref, True)

        @pl.when(qi % 2 == 1)
        def _():
            scores(qi, s1_ref)
            absorb(qi - 1, s0_ref, False)
            absorb(qi, s1_ref, True)

        o_t = jnp.concatenate([acc_ref[h, :B_V_DIM] / acc_ref[h, B_V_DIM:B_V_DIM + 1] for h in heads], axis=0)
        o_ref[0, pl.ds(pl.multiple_of(qi * tq, tq), tq), :] = o_t.T.astype(o_ref.dtype)
        return carry

    lax.fori_loop(0, n_qblk, query_block, 0)


def _mla_attn(qn_t, qr_t, kn, kr, v_t):
    bsz, n_blk, _, _ = qn_t.shape
    seq = n_blk * MLA_TQ
    n_groups = B_HEADS // MLA_GROUP
    gn, gr, gv = MLA_GROUP * B_NOPE_DIM, MLA_GROUP * B_ROPE_DIM, MLA_GROUP * B_V_DIM
    assert MLA_TQ == MLA_TK and MLA_GROUP % 2 == 0 and gn % LANES == 0 and gr % SUBLANES == 0
    score_buf = pltpu.VMEM((MLA_GROUP, MLA_TK, MLA_TQ), F32)
    blocked = lambda n: pl.BlockSpec((1, n_blk, n, MLA_TK), lambda b, g: (b, 0, g, 0))
    return pl.pallas_call(
        functools.partial(_mla_attn_kernel, n_qblk=n_blk),
        grid=(bsz, n_groups),
        in_specs=[blocked(gn), blocked(gr),
                  pl.BlockSpec((1, seq, gn), lambda b, g: (b, 0, g)),
                  pl.BlockSpec((1, seq, LANES), lambda b, g: (b, 0, 0)),
                  blocked(gv)],
        out_specs=pl.BlockSpec((1, seq, gv), lambda b, g: (b, 0, g)),
        out_shape=jax.ShapeDtypeStruct((bsz, seq, B_HEADS * B_V_DIM), BF16),
        scratch_shapes=[pltpu.VMEM((MLA_GROUP, 2 * LANES, MLA_TQ), BF16),
                        score_buf, score_buf,
                        pltpu.VMEM((MLA_GROUP, 1, MLA_TQ), F32),
                        pltpu.VMEM((MLA_GROUP, B_V_DIM + MLA_ONES, MLA_TQ), F32)],
        compiler_params=_params("parallel", "parallel"),
        name="mla_attn",
    )(qn_t, qr_t, kn, kr, v_t)


def _rope_tables(seq):
    inv_freq = ROPE_BASE ** (-jnp.arange(0, B_ROPE_DIM, 2, dtype=F32) / B_ROPE_DIM)
    ang = jnp.arange(seq, dtype=F32)[:, None] * inv_freq[None, :]
    cos, sin = jnp.cos(ang), jnp.sin(ang)
    reps = LANES // B_ROPE_DIM
    cos = jnp.tile(jnp.concatenate([cos, cos], axis=1), (1, reps))
    sin = jnp.tile(jnp.concatenate([-sin, sin], axis=1), (1, reps))
    return cos, sin, cos.T, sin.T


def kernel(x, a_w_qkv, a_w_o, kv_w_dkv, kv_norm_g, kv_w_kr, kv_w_uk, kv_w_uv, b_w_dq, b_q_norm_g, b_w_uq, b_w_o, ffn_w_in, ffn_conv_w, ffn_conv_b, ffn_w_out, ln_mix_g, ln_mix_b, ln_ffn_g, ln_ffn_b):
    bsz, seq, d = x.shape
    t = bsz * seq
    x2d = x.reshape(t, d)
    row = lambda a: a.reshape(1, -1)

    def tail(o, xin, w_o, layer):
        return _layer_tail(o.reshape(t, -1), xin, seq, w_o.astype(BF16), row(ln_mix_g[layer]), row(ln_mix_b[layer]),
                           ffn_w_in[layer].astype(BF16), ffn_conv_w[layer], row(ffn_conv_b[layer]),
                           ffn_w_out[layer].astype(BF16), row(ln_ffn_g[layer]), row(ln_ffn_b[layer]))

    slopes = jnp.asarray([2.0 ** (-8.0 * (h + 1) / A_HEADS) for h in range(A_HEADS)], dtype=F32)
    qkv = _qkv_proj(x2d, a_w_qkv[0].astype(BF16))
    o = _dilated_attn(qkv.reshape(bsz, seq, -1), slopes)
    x2d = tail(o, x2d, a_w_o[0], 0)

    cos, sin, cos_t, sin_t = _rope_tables(seq)
    w_down = jnp.concatenate([kv_w_dkv, jnp.tile(kv_w_kr, (1, LANES // B_ROPE_DIM)), b_w_dq[0]], axis=1)
    w_uq = b_w_uq[0].reshape(Q_RANK, B_HEADS, B_NOPE_DIM + B_ROPE_DIM)
    w_uqn_t = w_uq[:, :, :B_NOPE_DIM].reshape(Q_RANK, -1).T
    w_uqr_t = w_uq[:, :, B_NOPE_DIM:].reshape(Q_RANK, -1).T
    qn_t, qr_t, kn, kr, v_t = _mla_proj(x2d, seq, w_down.astype(BF16), row(kv_norm_g), row(b_q_norm_g[0]),
                                        kv_w_uk.astype(BF16), kv_w_uv.T.astype(BF16), w_uqn_t.astype(BF16),
                                        w_uqr_t.astype(BF16), cos, sin, cos_t, sin_t)
    shp = lambda a: a.reshape(bsz, seq, -1)
    o = _mla_attn(qn_t, qr_t, shp(kn), shp(kr), v_t)
    x2d = tail(o, x2d, b_w_o[0], 1)
    return x2d.reshape(bsz, seq, d)
```

```python
import functools

import jax
import jax.numpy as jnp
from jax import lax
from jax.experimental import pallas as pl
from jax.experimental.pallas import tpu as pltpu

D_MODEL = 1024
DEPTH = 2
A_HEADS = 16
A_HEAD_DIM = 64
DILATED_PATTERNS = ((128, 1), (512, 4), (2048, 16))
BAND = 128
B_HEADS = 16
B_NOPE_DIM = 64
B_ROPE_DIM = 32
B_V_DIM = 64
Q_RANK = 384
KV_RANK = 256
ROPE_BASE = 10000.0
D_FF = 2816
CONV_WIDTH = 3
ALPHA = (2.0 * DEPTH) ** 0.25
LN_EPS = 1e-5
RMS_EPS = 1e-6

LANES = 128
SUBLANES = 8
NEG = -1e30
LOG2E = 1.4426950408889634
VMEM_LIMIT = 56 * 1024 * 1024

F32 = jnp.float32
BF16 = jnp.bfloat16


def _const_spec(shape):
    nd = len(shape)
    return pl.BlockSpec(shape, lambda *_: (0,) * nd, pipeline_mode=pl.Buffered(1))


def _params(*sem):
    return pltpu.CompilerParams(dimension_semantics=sem, vmem_limit_bytes=VMEM_LIMIT)


def _layer_norm(y, g, b):
    mu = jnp.mean(y, axis=-1, keepdims=True)
    yc = y - mu
    var = jnp.mean(yc * yc, axis=-1, keepdims=True)
    return yc * lax.rsqrt(var + LN_EPS) * g + b


def _qkv_kernel(x_ref, w_ref, o_ref):
    o_ref[...] = jnp.dot(x_ref[...].astype(BF16), w_ref[...], preferred_element_type=F32)


def _qkv_proj(x2d, w):
    t, d = x2d.shape
    n = w.shape[1]
    tm = 512
    return pl.pallas_call(
        _qkv_kernel,
        grid=(t // tm,),
        in_specs=[pl.BlockSpec((tm, d), lambda i: (i, 0)), _const_spec((d, n))],
        out_specs=pl.BlockSpec((tm, n), lambda i: (i, 0)),
        out_shape=jax.ShapeDtypeStruct((t, n), F32),
        compiler_params=_params("parallel"),
        name="qkv_proj",
    )(x2d, w)


DIL_GROUP_FIRST = 8
DIL_GROUP_RESIDUES = 2
DIL_GROUP_LATER = 8


def _dilated_kernel(slopes_ref, q_ref, k_ref, v_ref, o_ref, ob_ref, lse_ref, bias_ref, *, seq):
    hp = pl.program_id(1)
    lane = lax.broadcasted_iota(jnp.int32, (1, LANES), 1)
    head0 = lane < A_HEAD_DIM
    head_masks = (head0, jnp.logical_not(head0))
    q_scale = A_HEAD_DIM ** -0.5 * LOG2E
    ones = jnp.ones((1, LANES), BF16)

    def fill_bias(dilation):
        for off, width in ((0, BAND), (BAND, 2 * BAND)):
            ql = lax.broadcasted_iota(jnp.int32, (BAND, width), 0)
            kl = lax.broadcasted_iota(jnp.int32, (BAND, width), 1)
            back = ql - kl + (width - BAND)
            valid = (back >= 0) & (back <= BAND)
            dist = (back * dilation).astype(F32)
            for hh in range(2):
                slope = slopes_ref[2 * hp + hh] * LOG2E
                bias_ref[hh, :, off:off + width] = jnp.where(valid, -slope * dist, NEG)

    def load_unit(dilation, q_start, k_start, width):
        q = (q_ref[0, pl.ds(q_start, BAND, stride=dilation), :] * q_scale).astype(BF16)
        k = k_ref[0, pl.ds(k_start, width, stride=dilation), :].astype(BF16)
        v = v_ref[0, pl.ds(k_start, width, stride=dilation), :].astype(BF16)
        return q, k, v

    def scores(q, k):
        width = k.shape[0]
        boff = 0 if width == BAND else BAND
        qs = jnp.concatenate([jnp.where(mask, q, jnp.zeros_like(q)) for mask in head_masks], axis=0)
        s = lax.dot_general(qs, k, (((1,), (1,)), ((), ())), preferred_element_type=F32)
        return [s[hh * BAND:(hh + 1) * BAND] + bias_ref[hh, :, boff:boff + width] for hh in range(2)]

    def softmax(s):
        m = jnp.max(s, axis=-1, keepdims=True)
        return m, jnp.exp2(s - m).astype(BF16)

    def weighted(p, v, hh):
        return jnp.dot(p, jnp.where(head_masks[hh], v, ones), preferred_element_type=F32)

    def run_group(branch, dilation, units):
        loaded = [load_unit(dilation, *u) for u in units]
        s_all = [scores(q, k) for q, k, _ in loaded]
        mp_all = [[softmax(s) for s in s_u] for s_u in s_all]
        pv_all = [[weighted(mp[hh][1], v, hh) for hh in range(2)] for mp, (_, _, v) in zip(mp_all, loaded)]
        for (q_start, _, _), mp, pv in zip(units, mp_all, pv_all):
            num = jnp.where(head0, pv[0], pv[1])
            den = pltpu.roll(jnp.where(head0, pv[1], pv[0]), A_HEAD_DIM, 1)
            rows = pl.ds(q_start, BAND, stride=dilation)
            ob_ref[branch, rows, :] = num / den
            lse_ref[branch, rows, :] = jnp.where(head0, mp[0][0], mp[1][0]) + jnp.log2(den)

    for branch, (_, dilation) in enumerate(DILATED_PATTERNS):
        n_blk = seq // dilation // BAND
        blk_stride = BAND * dilation
        fill_bias(dilation)
        first = lambda r: (r, r, BAND)
        later = lambda r, n, blk_stride=blk_stride: (n * blk_stride + r, (n - 1) * blk_stride + r, 2 * BAND)

        if n_blk == 1:
            group = DIL_GROUP_FIRST
            def body(g, c, branch=branch, dilation=dilation, group=group):
                run_group(branch, dilation, [first(g * group + u) for u in range(group)])
                return c
            lax.fori_loop(0, dilation // group, body, 0)
        elif dilation > 1:
            res = DIL_GROUP_RESIDUES
            def body(g, c, branch=branch, dilation=dilation, n_blk=n_blk, res=res):
                units = []
                for u in range(res):
                    r = g * res + u
                    units += [first(r)] + [later(r, n) for n in range(1, n_blk)]
                run_group(branch, dilation, units)
                return c
            lax.fori_loop(0, dilation // res, body, 0)
        else:
            per = DIL_GROUP_LATER
            assert (n_blk - per) % per == 0
            run_group(branch, dilation, [first(0)] + [later(0, n) for n in range(1, per)])

            def body(i, c, branch=branch, dilation=dilation, per=per):
                run_group(branch, dilation, [later(0, per * (i + 1) + u) for u in range(per)])
                return c
            lax.fori_loop(0, (n_blk - per) // per, body, 0)

    rows_per = 256

    def merge(i, carry):
        rows = pl.ds(pl.multiple_of(i * rows_per, rows_per), rows_per)
        l0, l1, l2 = lse_ref[0, rows, :], lse_ref[1, rows, :], lse_ref[2, rows, :]
        mx = jnp.maximum(jnp.maximum(l0, l1), l2)
        w0, w1, w2 = jnp.exp2(l0 - mx), jnp.exp2(l1 - mx), jnp.exp2(l2 - mx)
        num = w0 * ob_ref[0, rows, :] + w1 * ob_ref[1, rows, :] + w2 * ob_ref[2, rows, :]
        o_ref[0, rows, :] = (num / (w0 + w1 + w2)).astype(o_ref.dtype)
        return carry

    lax.fori_loop(0, seq // rows_per, merge, 0)


def _dilated_attn(qkv, slopes):
    bsz, seq, _ = qkv.shape
    n_pairs = A_HEADS * A_HEAD_DIM // LANES
    assert all(w // d == BAND and seq % (d * BAND) == 0 for w, d in DILATED_PATTERNS)

    def col_spec(part):
        return pl.BlockSpec((1, seq, LANES), lambda b, h, part=part: (b, 0, part * n_pairs + h))

    return pl.pallas_call(
        functools.partial(_dilated_kernel, seq=seq),
        grid=(bsz, n_pairs),
        in_specs=[pl.BlockSpec(memory_space=pltpu.SMEM), col_spec(0), col_spec(1), col_spec(2)],
        out_specs=pl.BlockSpec((1, seq, LANES), lambda b, h: (b, 0, h)),
        out_shape=jax.ShapeDtypeStruct((bsz, seq, A_HEADS * A_HEAD_DIM), BF16),
        scratch_shapes=[pltpu.VMEM((3, seq, LANES), F32), pltpu.VMEM((3, seq, LANES), F32),
                        pltpu.VMEM((2, BAND, 3 * BAND), F32)],
        compiler_params=_params("parallel", "parallel"),
        name="dilated_attn",
    )(slopes, qkv, qkv, qkv)


FFN_COLS = 256


def _tail_kernel(o_ref, x_ref, wo_ref, gm_ref, bm_ref, win_ref, cw_ref, cb_ref, wout_ref, gf_ref, bf_ref,
                 y_ref, u_ref, h_ref, *, tm, tiles_per_seq):
    i = pl.program_id(0)
    halo = SUBLANES

    @pl.when(i % tiles_per_seq == 0)
    def _():
        u_ref[0:halo, :] = jnp.zeros((halo, 2 * D_FF), F32)

    mix = jnp.dot(o_ref[...], wo_ref[...], preferred_element_type=F32)
    x = _layer_norm(ALPHA * x_ref[...] + mix, gm_ref[...], bm_ref[...])
    u_ref[halo:halo + tm, :] = jnp.dot(x.astype(BF16), win_ref[...], preferred_element_type=F32)

    for c in range(D_FF // FFN_COLS):
        def conv(col):
            cols = slice(col, col + FFN_COLS)
            acc = cb_ref[:, cols] + cw_ref[2:3, cols] * u_ref[halo:halo + tm, cols]
            acc = acc + cw_ref[1:2, cols] * u_ref[halo - 1:halo - 1 + tm, cols]
            return acc + cw_ref[0:1, cols] * u_ref[halo - 2:halo - 2 + tm, cols]

        gate = conv(c * FFN_COLS)
        val = conv(D_FF + c * FFN_COLS)
        h_ref[:, c * FFN_COLS:(c + 1) * FFN_COLS] = (jax.nn.silu(gate) * val).astype(BF16)

    u_ref[0:halo, :] = u_ref[tm:tm + halo, :]

    f = jnp.dot(h_ref[...], wout_ref[...], preferred_element_type=F32)
    y_ref[...] = _layer_norm(ALPHA * x + f, gf_ref[...], bf_ref[...])


def _layer_tail(o2d, x2d, seq, w_o, g_mix, b_mix, w_in, conv_w, conv_b, w_out, g_ffn, b_ffn):
    t, d = x2d.shape
    k = o2d.shape[1]
    tm = 512
    assert seq % tm == 0 and D_FF % FFN_COLS == 0 and CONV_WIDTH - 1 <= SUBLANES
    return pl.pallas_call(
        functools.partial(_tail_kernel, tm=tm, tiles_per_seq=seq // tm),
        grid=(t // tm,),
        in_specs=[pl.BlockSpec((tm, k), lambda i: (i, 0)), pl.BlockSpec((tm, d), lambda i: (i, 0)),
                  _const_spec((k, d)), _const_spec((1, d)), _const_spec((1, d)),
                  _const_spec((d, 2 * D_FF)), _const_spec((CONV_WIDTH, 2 * D_FF)), _const_spec((1, 2 * D_FF)),
                  _const_spec((D_FF, d)), _const_spec((1, d)), _const_spec((1, d))],
        out_specs=pl.BlockSpec((tm, d), lambda i: (i, 0)),
        out_shape=jax.ShapeDtypeStruct((t, d), F32),
        scratch_shapes=[pltpu.VMEM((tm + SUBLANES, 2 * D_FF), F32), pltpu.VMEM((tm, D_FF), BF16)],
        compiler_params=_params("arbitrary"),
        name="layer_tail",
    )(o2d, x2d, w_o, g_mix, b_mix, w_in, conv_w, conv_b, w_out, g_ffn, b_ffn)


def _rope(t, cos, sin_signed):
    lane = lax.broadcasted_iota(jnp.int32, t.shape, 1)
    half = B_ROPE_DIM // 2
    first = (lane % B_ROPE_DIM) < half
    swapped = jnp.where(first, pltpu.roll(t, LANES - half, 1), pltpu.roll(t, half, 1))
    return t * cos + swapped * sin_signed


def _rope_t(t, cos, sin_signed):
    half = B_ROPE_DIM // 2
    pieces = [t[r * half:(r + 1) * half] for r in range(t.shape[0] // half)]
    swapped = jnp.concatenate([pieces[r ^ 1] for r in range(len(pieces))], axis=0)
    return t * cos + swapped * sin_signed


def _rms(c, g):
    return c * lax.rsqrt(jnp.mean(c * c, axis=-1, keepdims=True) + RMS_EPS) * g


def _mla_proj_kernel(x_ref, wd_ref, kvg_ref, qg_ref, wuk_ref, wuvt_ref, wuqnt_ref, wuqrt_ref,
                     cos_ref, sin_ref, cost_ref, sint_ref, qn_ref, qr_ref, kn_ref, kr_ref, v_ref):
    nt = (((1,), (1,)), ((), ()))
    x = x_ref[...].astype(BF16)
    down = jnp.dot(x, wd_ref[...], preferred_element_type=F32)

    c_kv = _rms(down[:, :KV_RANK], kvg_ref[...]).astype(BF16)
    kn_ref[...] = jnp.dot(c_kv, wuk_ref[...], preferred_element_type=F32).astype(BF16)
    kr_ref[...] = _rope(down[:, KV_RANK:KV_RANK + LANES], cos_ref[...], sin_ref[...]).astype(BF16)
    v_t = lax.dot_general(wuvt_ref[...], c_kv, nt, preferred_element_type=F32)
    for c in range(v_t.shape[1] // MLA_TK):
        v_ref[0, c] = v_t[:, c * MLA_TK:(c + 1) * MLA_TK].astype(BF16)

    scale = (B_NOPE_DIM + B_ROPE_DIM) ** -0.5 * LOG2E
    c_q = _rms(down[:, KV_RANK + LANES:], qg_ref[...]).astype(BF16)
    qn_t = (lax.dot_general(wuqnt_ref[...], c_q, nt, preferred_element_type=F32) * scale).astype(BF16)
    qr_t = lax.dot_general(wuqrt_ref[...], c_q, nt, preferred_element_type=F32)
    cos_t, sin_t = cost_ref[...], sint_ref[...]
    qr_t = jnp.concatenate([(_rope_t(qr_t[c * LANES:(c + 1) * LANES], cos_t, sin_t) * scale).astype(BF16)
                            for c in range(qr_t.shape[0] // LANES)], axis=0)
    for c in range(qn_t.shape[1] // MLA_TQ):
        cols = slice(c * MLA_TQ, (c + 1) * MLA_TQ)
        qn_ref[0, c] = qn_t[:, cols]
        qr_ref[0, c] = qr_t[:, cols]


def _mla_proj(x2d, seq, w_down, kv_g, q_g, w_uk, w_uv_t, w_uqn_t, w_uqr_t, cos, sin, cos_t, sin_t):
    t, d = x2d.shape
    tm = 512
    n_seq_tiles = seq // tm
    bsz = t // seq
    blk = tm // MLA_TK
    hn, hr, hv = B_HEADS * B_NOPE_DIM, B_HEADS * B_ROPE_DIM, B_HEADS * B_V_DIM
    row = lambda n: pl.BlockSpec((tm, n), lambda i: (i, 0))
    tab = pl.BlockSpec((tm, LANES), lambda i: (i % n_seq_tiles, 0))
    tab_t = pl.BlockSpec((LANES, tm), lambda i: (0, i % n_seq_tiles))
    feat_t = lambda n: pl.BlockSpec((1, blk, n, MLA_TK), lambda i: (i // n_seq_tiles, i % n_seq_tiles, 0, 0))
    feat_shape = lambda n: jax.ShapeDtypeStruct((bsz, seq // MLA_TK, n, MLA_TK), BF16)
    return pl.pallas_call(
        _mla_proj_kernel,
        grid=(t // tm,),
        in_specs=[row(d), _const_spec(w_down.shape), _const_spec((1, KV_RANK)), _const_spec((1, Q_RANK)),
                  _const_spec(w_uk.shape), _const_spec(w_uv_t.shape), _const_spec(w_uqn_t.shape),
                  _const_spec(w_uqr_t.shape), tab, tab, tab_t, tab_t],
        out_specs=[feat_t(hn), feat_t(hr), row(hn), row(LANES), feat_t(hv)],
        out_shape=[feat_shape(hn), feat_shape(hr),
                   jax.ShapeDtypeStruct((t, hn), BF16), jax.ShapeDtypeStruct((t, LANES), BF16), feat_shape(hv)],
        compiler_params=_params("parallel"),
        name="mla_proj",
    )(x2d, w_down, kv_g, q_g, w_uk, w_uv_t, w_uqn_t, w_uqr_t, cos, sin, cos_t, sin_t)


MLA_GROUP = 8
MLA_QUNROLL = 4
MLA_TQ = 256
MLA_TK = 256
MLA_ONES = 16


def _mla_attn_kernel(qn_ref, qr_ref, kn_ref, kr_ref, v_ref, o_ref, qt_ref, s0_ref, s1_ref, m_ref, acc_ref,
                     *, n_qblk):
    tq, tk = MLA_TQ, MLA_TK
    heads = range(MLA_GROUP)
    ones = jnp.ones((MLA_ONES, tk), BF16)
    bufs = (s0_ref, s1_ref)

    def build_query(qi, slot):
        rope_slots = LANES // B_ROPE_DIM
        for h in heads:
            zn = jnp.zeros((B_NOPE_DIM, tq), BF16)
            zr = jnp.zeros((B_ROPE_DIM, tq), BF16)
            qn = qn_ref[0, qi, h * B_NOPE_DIM:(h + 1) * B_NOPE_DIM, :]
            qr = qr_ref[0, qi, h * B_ROPE_DIM:(h + 1) * B_ROPE_DIM, :]
            qt_ref[slot, h] = jnp.concatenate([qn, zn][::1 if h % 2 == 0 else -1]
                                              + [qr if g == h % rope_slots else zr for g in range(rope_slots)],
                                              axis=0)

    def scores(j, s_ref, slot):
        ks = pl.multiple_of(j * tk, tk)
        kr = kr_ref[0, pl.ds(ks, tk), :]
        for pair in range(MLA_GROUP // 2):
            k_cat = jnp.concatenate([kn_ref[0, pl.ds(ks, tk), pair * LANES:(pair + 1) * LANES], kr], axis=1)
            for h in (2 * pair, 2 * pair + 1):
                s_ref[h] = jnp.dot(k_cat, qt_ref[slot, h], preferred_element_type=F32)

    def absorb(j, s_ref, masked):
        if masked:
            kl = lax.broadcasted_iota(jnp.int32, (tk, tq), 0)
            ql = lax.broadcasted_iota(jnp.int32, (tk, tq), 1)
            causal = kl <= ql
        mp_all = []
        for h in heads:
            s, m_old = s_ref[h], m_ref[h]
            if masked:
                s = jnp.where(causal, s, NEG)
            m_new = jnp.maximum(m_old, jnp.max(s, axis=0, keepdims=True))
            mp_all.append((m_old, m_new, jnp.exp2(s - m_new).astype(BF16)))
        for h in heads:
            m_old, m_new, p = mp_all[h]
            v1 = jnp.concatenate([v_ref[0, j, h * B_V_DIM:(h + 1) * B_V_DIM, :], ones], axis=0)
            pv = jnp.dot(v1, p, preferred_element_type=F32)
            acc_ref[h] = jnp.exp2(m_old - m_new) * acc_ref[h] + pv
            m_ref[h] = m_new

    def query_block(qi, odd, first, slot):
        cur, other = bufs[first], bufs[1 - first]
        nxt = jnp.minimum(qi + 1, n_qblk - 1)
        m_ref[...] = jnp.full(m_ref.shape, NEG, F32)
        acc_ref[...] = jnp.zeros(acc_ref.shape, F32)

        def two_blocks(i, carry):
            j = 2 * i
            scores(j + 1, other, slot)
            absorb(j, cur, False)
            scores(j + 2, cur, slot)
            absorb(j + 1, other, False)
            return carry

        lax.fori_loop(0, qi // 2, two_blocks, 0)
        build_query(nxt, 1 - slot)
        if odd:
            scores(qi, other, slot)
            absorb(qi - 1, cur, False)
            scores(0, cur, 1 - slot)
            absorb(qi, other, True)
            free = first
        else:
            scores(0, other, 1 - slot)
            absorb(qi, cur, True)
            free = 1 - first

        o_t = jnp.concatenate([acc_ref[h, :B_V_DIM] / acc_ref[h, B_V_DIM:B_V_DIM + 1] for h in heads], axis=0)
        o_ref[0, pl.ds(pl.multiple_of(qi * tq, tq), tq), :] = o_t.T.astype(o_ref.dtype)
        return free

    build_query(0, 0)
    scores(0, bufs[0], 0)

    def query_blocks(u, carry):
        first = 0
        for k in range(MLA_QUNROLL):
            first = query_block(u * MLA_QUNROLL + k, k % 2 == 1, first, k % 2)
        assert first == 0
        return carry

    lax.fori_loop(0, n_qblk // MLA_QUNROLL, query_blocks, 0)


def _mla_attn(qn_t, qr_t, kn, kr, v_t):
    bsz, n_blk, _, _ = qn_t.shape
    seq = n_blk * MLA_TQ
    n_groups = B_HEADS // MLA_GROUP
    gn, gr, gv = MLA_GROUP * B_NOPE_DIM, MLA_GROUP * B_ROPE_DIM, MLA_GROUP * B_V_DIM
    assert MLA_TQ == MLA_TK and MLA_GROUP % 2 == 0 and gn % LANES == 0 and gr % SUBLANES == 0
    assert n_blk % MLA_QUNROLL == 0
    score_buf = pltpu.VMEM((MLA_GROUP, MLA_TK, MLA_TQ), F32)
    blocked = lambda n: pl.BlockSpec((1, n_blk, n, MLA_TK), lambda b, g: (b, 0, g, 0))
    return pl.pallas_call(
        functools.partial(_mla_attn_kernel, n_qblk=n_blk),
        grid=(bsz, n_groups),
        in_specs=[blocked(gn), blocked(gr),
                  pl.BlockSpec((1, seq, gn), lambda b, g: (b, 0, g)),
                  pl.BlockSpec((1, seq, LANES), lambda b, g: (b, 0, 0)),
                  blocked(gv)],
        out_specs=pl.BlockSpec((1, seq, gv), lambda b, g: (b, 0, g)),
        out_shape=jax.ShapeDtypeStruct((bsz, seq, B_HEADS * B_V_DIM), BF16),
        scratch_shapes=[pltpu.VMEM((2, MLA_GROUP, 2 * LANES, MLA_TQ), BF16),
                        score_buf, score_buf,
                        pltpu.VMEM((MLA_GROUP, 1, MLA_TQ), F32),
                        pltpu.VMEM((MLA_GROUP, B_V_DIM + MLA_ONES, MLA_TQ), F32)],
        compiler_params=_params("parallel", "parallel"),
        name="mla_attn",
    )(qn_t, qr_t, kn, kr, v_t)


def _rope_tables(seq):
    inv_freq = ROPE_BASE ** (-jnp.arange(0, B_ROPE_DIM, 2, dtype=F32) / B_ROPE_DIM)
    ang = jnp.arange(seq, dtype=F32)[:, None] * inv_freq[None, :]
    cos, sin = jnp.cos(ang), jnp.sin(ang)
    reps = LANES // B_ROPE_DIM
    cos = jnp.tile(jnp.concatenate([cos, cos], axis=1), (1, reps))
    sin = jnp.tile(jnp.concatenate([-sin, sin], axis=1), (1, reps))
    return cos, sin, cos.T, sin.T


def kernel(x, a_w_qkv, a_w_o, kv_w_dkv, kv_norm_g, kv_w_kr, kv_w_uk, kv_w_uv, b_w_dq, b_q_norm_g, b_w_uq, b_w_o, ffn_w_in, ffn_conv_w, ffn_conv_b, ffn_w_out, ln_mix_g, ln_mix_b, ln_ffn_g, ln_ffn_b):
    bsz, seq, d = x.shape
    t = bsz * seq
    x2d = x.reshape(t, d)
    row = lambda a: a.reshape(1, -1)

    def tail(o, xin, w_o, layer):
        return _layer_tail(o.reshape(t, -1), xin, seq, w_o.astype(BF16), row(ln_mix_g[layer]), row(ln_mix_b[layer]),
                           ffn_w_in[layer].astype(BF16), ffn_conv_w[layer], row(ffn_conv_b[layer]),
                           ffn_w_out[layer].astype(BF16), row(ln_ffn_g[layer]), row(ln_ffn_b[layer]))

    slopes = jnp.asarray([2.0 ** (-8.0 * (h + 1) / A_HEADS) for h in range(A_HEADS)], dtype=F32)
    qkv = _qkv_proj(x2d, a_w_qkv[0].astype(BF16))
    o = _dilated_attn(qkv.reshape(bsz, seq, -1), slopes)
    x2d = tail(o, x2d, a_w_o[0], 0)

    cos, sin, cos_t, sin_t = _rope_tables(seq)
    w_down = jnp.concatenate([kv_w_dkv, jnp.tile(kv_w_kr, (1, LANES // B_ROPE_DIM)), b_w_dq[0]], axis=1)
    w_uq = b_w_uq[0].reshape(Q_RANK, B_HEADS, B_NOPE_DIM + B_ROPE_DIM)
    w_uqn_t = w_uq[:, :, :B_NOPE_DIM].reshape(Q_RANK, -1).T
    w_uqr_t = w_uq[:, :, B_NOPE_DIM:].reshape(Q_RANK, -1).T
    qn_t, qr_t, kn, kr, v_t = _mla_proj(x2d, seq, w_down.astype(BF16), row(kv_norm_g), row(b_q_norm_g[0]),
                                        kv_w_uk.astype(BF16), kv_w_uv.T.astype(BF16), w_uqn_t.astype(BF16),
                                        w_uqr_t.astype(BF16), cos, sin, cos_t, sin_t)
    shp = lambda a: a.reshape(bsz, seq, -1)
    o = _mla_attn(qn_t, qr_t, shp(kn), shp(kr), v_t)
    x2d = tail(o, x2d, b_w_o[0], 1)
    return x2d.reshape(bsz, seq, d)
```

```python
import functools

import jax
import jax.numpy as jnp
from jax import lax
from jax.experimental import pallas as pl
from jax.experimental.pallas import tpu as pltpu

D_MODEL = 1024
DEPTH = 2
A_HEADS = 16
A_HEAD_DIM = 64
DILATED_PATTERNS = ((128, 1), (512, 4), (2048, 16))
BAND = 128
B_HEADS = 16
B_NOPE_DIM = 64
B_ROPE_DIM = 32
B_V_DIM = 64
Q_RANK = 384
KV_RANK = 256
ROPE_BASE = 10000.0
D_FF = 2816
CONV_WIDTH = 3
ALPHA = (2.0 * DEPTH) ** 0.25
LN_EPS = 1e-5
RMS_EPS = 1e-6

LANES = 128
SUBLANES = 8
NEG = -1e30
LOG2E = 1.4426950408889634
VMEM_LIMIT = 56 * 1024 * 1024

F32 = jnp.float32
BF16 = jnp.bfloat16


def _const_spec(shape):
    nd = len(shape)
    return pl.BlockSpec(shape, lambda *_: (0,) * nd, pipeline_mode=pl.Buffered(1))


def _layer_spec(shape, layer):
    return pl.BlockSpec((None,) + shape, lambda *_: (layer,) + (0,) * len(shape), pipeline_mode=pl.Buffered(1))


def _params(*sem):
    return pltpu.CompilerParams(dimension_semantics=sem, vmem_limit_bytes=VMEM_LIMIT)


def _layer_norm(y, g, b):
    mu = jnp.mean(y, axis=-1, keepdims=True)
    yc = y - mu
    var = jnp.mean(yc * yc, axis=-1, keepdims=True)
    return yc * lax.rsqrt(var + LN_EPS) * g + b


def _qkv_kernel(x_ref, w_ref, o_ref):
    o_ref[...] = jnp.dot(x_ref[...].astype(BF16), w_ref[...], preferred_element_type=F32)


def _qkv_proj(x2d, w):
    t, d = x2d.shape
    n = w.shape[1]
    tm = 512
    return pl.pallas_call(
        _qkv_kernel,
        grid=(t // tm,),
        in_specs=[pl.BlockSpec((tm, d), lambda i: (i, 0)), _const_spec((d, n))],
        out_specs=pl.BlockSpec((tm, n), lambda i: (i, 0)),
        out_shape=jax.ShapeDtypeStruct((t, n), F32),
        compiler_params=_params("parallel"),
        name="qkv_proj",
    )(x2d, w)


DIL_GROUP_FIRST = 8
DIL_GROUP_RESIDUES = 2
DIL_GROUP_LATER = 8


def _dilated_kernel(slopes_ref, q_ref, k_ref, v_ref, o_ref, ob_ref, lse_ref, bias_ref, *, seq):
    hp = pl.program_id(1)
    lane = lax.broadcasted_iota(jnp.int32, (1, LANES), 1)
    head0 = lane < A_HEAD_DIM
    head_masks = (head0, jnp.logical_not(head0))
    q_scale = A_HEAD_DIM ** -0.5 * LOG2E
    ones = jnp.ones((1, LANES), BF16)

    def fill_bias(dilation):
        for off, width in ((0, BAND), (BAND, 2 * BAND)):
            ql = lax.broadcasted_iota(jnp.int32, (BAND, width), 0)
            kl = lax.broadcasted_iota(jnp.int32, (BAND, width), 1)
            back = ql - kl + (width - BAND)
            valid = (back >= 0) & (back <= BAND)
            dist = (back * dilation).astype(F32)
            for hh in range(2):
                slope = slopes_ref[2 * hp + hh] * LOG2E
                bias_ref[hh, :, off:off + width] = jnp.where(valid, -slope * dist, NEG)

    def load_unit(dilation, q_start, k_start, width):
        q = (q_ref[0, pl.ds(q_start, BAND, stride=dilation), :] * q_scale).astype(BF16)
        k = k_ref[0, pl.ds(k_start, width, stride=dilation), :].astype(BF16)
        v = v_ref[0, pl.ds(k_start, width, stride=dilation), :].astype(BF16)
        return q, k, v

    def scores(q, k):
        width = k.shape[0]
        boff = 0 if width == BAND else BAND
        qs = jnp.concatenate([jnp.where(mask, q, jnp.zeros_like(q)) for mask in head_masks], axis=0)
        s = lax.dot_general(qs, k, (((1,), (1,)), ((), ())), preferred_element_type=F32)
        return [s[hh * BAND:(hh + 1) * BAND] + bias_ref[hh, :, boff:boff + width] for hh in range(2)]

    def softmax(s):
        m = jnp.max(s, axis=-1, keepdims=True)
        return m, jnp.exp2(s - m).astype(BF16)

    def weighted(p, v, hh):
        return jnp.dot(p, jnp.where(head_masks[hh], v, ones), preferred_element_type=F32)

    def run_group(branch, dilation, units):
        loaded = [load_unit(dilation, *u) for u in units]
        s_all = [scores(q, k) for q, k, _ in loaded]
        mp_all = [[softmax(s) for s in s_u] for s_u in s_all]
        pv_all = [[weighted(mp[hh][1], v, hh) for hh in range(2)] for mp, (_, _, v) in zip(mp_all, loaded)]
        for (q_start, _, _), mp, pv in zip(units, mp_all, pv_all):
            num = jnp.where(head0, pv[0], pv[1])
            den = pltpu.roll(jnp.where(head0, pv[1], pv[0]), A_HEAD_DIM, 1)
            rows = pl.ds(q_start, BAND, stride=dilation)
            ob_ref[branch, rows, :] = num / den
            lse_ref[branch, rows, :] = jnp.where(head0, mp[0][0], mp[1][0]) + jnp.log2(den)

    for branch, (_, dilation) in enumerate(DILATED_PATTERNS):
        n_blk = seq // dilation // BAND
        blk_stride = BAND * dilation
        fill_bias(dilation)
        first = lambda r: (r, r, BAND)
        later = lambda r, n, blk_stride=blk_stride: (n * blk_stride + r, (n - 1) * blk_stride + r, 2 * BAND)

        if n_blk == 1:
            group = DIL_GROUP_FIRST
            def body(g, c, branch=branch, dilation=dilation, group=group):
                run_group(branch, dilation, [first(g * group + u) for u in range(group)])
                return c
            lax.fori_loop(0, dilation // group, body, 0)
        elif dilation > 1:
            res = DIL_GROUP_RESIDUES
            def body(g, c, branch=branch, dilation=dilation, n_blk=n_blk, res=res):
                units = []
                for u in range(res):
                    r = g * res + u
                    units += [first(r)] + [later(r, n) for n in range(1, n_blk)]
                run_group(branch, dilation, units)
                return c
            lax.fori_loop(0, dilation // res, body, 0)
        else:
            per = DIL_GROUP_LATER
            assert (n_blk - per) % per == 0
            run_group(branch, dilation, [first(0)] + [later(0, n) for n in range(1, per)])

            def body(i, c, branch=branch, dilation=dilation, per=per):
                run_group(branch, dilation, [later(0, per * (i + 1) + u) for u in range(per)])
                return c
            lax.fori_loop(0, (n_blk - per) // per, body, 0)

    rows_per = 256

    def merge(i, carry):
        rows = pl.ds(pl.multiple_of(i * rows_per, rows_per), rows_per)
        l0, l1, l2 = lse_ref[0, rows, :], lse_ref[1, rows, :], lse_ref[2, rows, :]
        mx = jnp.maximum(jnp.maximum(l0, l1), l2)
        w0, w1, w2 = jnp.exp2(l0 - mx), jnp.exp2(l1 - mx), jnp.exp2(l2 - mx)
        num = w0 * ob_ref[0, rows, :] + w1 * ob_ref[1, rows, :] + w2 * ob_ref[2, rows, :]
        o_ref[0, rows, :] = (num / (w0 + w1 + w2)).astype(o_ref.dtype)
        return carry

    lax.fori_loop(0, seq // rows_per, merge, 0)


def _dilated_attn(qkv, slopes):
    bsz, seq, _ = qkv.shape
    n_pairs = A_HEADS * A_HEAD_DIM // LANES
    assert all(w // d == BAND and seq % (d * BAND) == 0 for w, d in DILATED_PATTERNS)

    def col_spec(part):
        return pl.BlockSpec((1, seq, LANES), lambda b, h, part=part: (b, 0, part * n_pairs + h))

    return pl.pallas_call(
        functools.partial(_dilated_kernel, seq=seq),
        grid=(bsz, n_pairs),
        in_specs=[pl.BlockSpec(memory_space=pltpu.SMEM), col_spec(0), col_spec(1), col_spec(2)],
        out_specs=pl.BlockSpec((1, seq, LANES), lambda b, h: (b, 0, h)),
        out_shape=jax.ShapeDtypeStruct((bsz, seq, A_HEADS * A_HEAD_DIM), BF16),
        scratch_shapes=[pltpu.VMEM((3, seq, LANES), F32), pltpu.VMEM((3, seq, LANES), F32),
                        pltpu.VMEM((2, BAND, 3 * BAND), F32)],
        compiler_params=_params("parallel", "parallel"),
        name="dilated_attn",
    )(slopes, qkv, qkv, qkv)


FFN_COLS = 256


def _tail_kernel(o_ref, x_ref, wo_ref, gm_ref, bm_ref, win_ref, cw_ref, cb_ref, wout_ref, gf_ref, bf_ref,
                 y_ref, u_ref, h_ref, *, tm, tiles_per_seq):
    i = pl.program_id(0)
    halo = SUBLANES

    @pl.when(i % tiles_per_seq == 0)
    def _():
        u_ref[0:halo, :] = jnp.zeros((halo, 2 * D_FF), F32)

    mix = jnp.dot(o_ref[...], wo_ref[...], preferred_element_type=F32)
    x = _layer_norm(ALPHA * x_ref[...] + mix, gm_ref[...], bm_ref[...])
    u_ref[halo:halo + tm, :] = jnp.dot(x.astype(BF16), win_ref[...], preferred_element_type=F32)

    for c in range(D_FF // FFN_COLS):
        def conv(col):
            cols = slice(col, col + FFN_COLS)
            acc = cb_ref[:, cols] + cw_ref[2:3, cols] * u_ref[halo:halo + tm, cols]
            acc = acc + cw_ref[1:2, cols] * u_ref[halo - 1:halo - 1 + tm, cols]
            return acc + cw_ref[0:1, cols] * u_ref[halo - 2:halo - 2 + tm, cols]

        gate = conv(c * FFN_COLS)
        val = conv(D_FF + c * FFN_COLS)
        h_ref[:, c * FFN_COLS:(c + 1) * FFN_COLS] = (jax.nn.silu(gate) * val).astype(BF16)

    u_ref[0:halo, :] = u_ref[tm:tm + halo, :]

    f = jnp.dot(h_ref[...], wout_ref[...], preferred_element_type=F32)
    y_ref[...] = _layer_norm(ALPHA * x + f, gf_ref[...], bf_ref[...])


def _layer_tail(o2d, x2d, seq, layer, w_o, g_mix, b_mix, w_in, conv_w, conv_b, w_out, g_ffn, b_ffn):
    t, d = x2d.shape
    k = o2d.shape[1]
    tm = 512
    assert seq % tm == 0 and D_FF % FFN_COLS == 0 and CONV_WIDTH - 1 <= SUBLANES
    return pl.pallas_call(
        functools.partial(_tail_kernel, tm=tm, tiles_per_seq=seq // tm),
        grid=(t // tm,),
        in_specs=[pl.BlockSpec((tm, k), lambda i: (i, 0)), pl.BlockSpec((tm, d), lambda i: (i, 0)),
                  _const_spec((k, d)), _const_spec((1, d)), _const_spec((1, d)),
                  _layer_spec((d, 2 * D_FF), layer), _const_spec((CONV_WIDTH, 2 * D_FF)), _const_spec((1, 2 * D_FF)),
                  _layer_spec((D_FF, d), layer), _const_spec((1, d)), _const_spec((1, d))],
        out_specs=pl.BlockSpec((tm, d), lambda i: (i, 0)),
        out_shape=jax.ShapeDtypeStruct((t, d), F32),
        scratch_shapes=[pltpu.VMEM((tm + SUBLANES, 2 * D_FF), F32), pltpu.VMEM((tm, D_FF), BF16)],
        compiler_params=_params("arbitrary"),
        name="layer_tail",
    )(o2d, x2d, w_o, g_mix, b_mix, w_in, conv_w, conv_b, w_out, g_ffn, b_ffn)


def _rope(t, cos, sin_signed):
    lane = lax.broadcasted_iota(jnp.int32, t.shape, 1)
    half = B_ROPE_DIM // 2
    first = (lane % B_ROPE_DIM) < half
    swapped = jnp.where(first, pltpu.roll(t, LANES - half, 1), pltpu.roll(t, half, 1))
    return t * cos + swapped * sin_signed


def _rope_t(t, cos, sin_signed):
    half = B_ROPE_DIM // 2
    pieces = [t[r * half:(r + 1) * half] for r in range(t.shape[0] // half)]
    swapped = jnp.concatenate([pieces[r ^ 1] for r in range(len(pieces))], axis=0)
    return t * cos + swapped * sin_signed


def _rms(c, g):
    return c * lax.rsqrt(jnp.mean(c * c, axis=-1, keepdims=True) + RMS_EPS) * g


def _mla_proj_kernel(x_ref, wd_ref, kvg_ref, qg_ref, wuk_ref, wuvt_ref, wuqnt_ref, wuqrt_ref,
                     cos_ref, sin_ref, cost_ref, sint_ref, qn_ref, qr_ref, kn_ref, kr_ref, v_ref):
    nt = (((1,), (1,)), ((), ()))
    x = x_ref[...].astype(BF16)
    down = jnp.dot(x, wd_ref[...], preferred_element_type=F32)

    c_kv = _rms(down[:, :KV_RANK], kvg_ref[...]).astype(BF16)
    kn_ref[...] = jnp.dot(c_kv, wuk_ref[...], preferred_element_type=F32).astype(BF16)
    kr_ref[...] = _rope(down[:, KV_RANK:KV_RANK + LANES], cos_ref[...], sin_ref[...]).astype(BF16)
    v_t = lax.dot_general(wuvt_ref[...], c_kv, nt, preferred_element_type=F32)
    for c in range(v_t.shape[1] // MLA_TK):
        v_ref[0, c] = v_t[:, c * MLA_TK:(c + 1) * MLA_TK].astype(BF16)

    scale = (B_NOPE_DIM + B_ROPE_DIM) ** -0.5 * LOG2E
    c_q = _rms(down[:, KV_RANK + LANES:], qg_ref[...]).astype(BF16)
    qn_t = (lax.dot_general(wuqnt_ref[...], c_q, nt, preferred_element_type=F32) * scale).astype(BF16)
    qr_t = lax.dot_general(wuqrt_ref[...], c_q, nt, preferred_element_type=F32)
    cos_t, sin_t = cost_ref[...], sint_ref[...]
    qr_t = jnp.concatenate([(_rope_t(qr_t[c * LANES:(c + 1) * LANES], cos_t, sin_t) * scale).astype(BF16)
                            for c in range(qr_t.shape[0] // LANES)], axis=0)
    for c in range(qn_t.shape[1] // MLA_TQ):
        cols = slice(c * MLA_TQ, (c + 1) * MLA_TQ)
        qn_ref[0, c] = qn_t[:, cols]
        qr_ref[0, c] = qr_t[:, cols]


def _mla_proj(x2d, seq, w_down, kv_g, q_g, w_uk, w_uv_t, w_uqn_t, w_uqr_t, cos, sin, cos_t, sin_t):
    t, d = x2d.shape
    tm = 512
    n_seq_tiles = seq // tm
    bsz = t // seq
    blk = tm // MLA_TK
    hn, hr, hv = B_HEADS * B_NOPE_DIM, B_HEADS * B_ROPE_DIM, B_HEADS * B_V_DIM
    row = lambda n: pl.BlockSpec((tm, n), lambda i: (i, 0))
    tab = pl.BlockSpec((tm, LANES), lambda i: (i % n_seq_tiles, 0))
    tab_t = pl.BlockSpec((LANES, tm), lambda i: (0, i % n_seq_tiles))
    feat_t = lambda n: pl.BlockSpec((1, blk, n, MLA_TK), lambda i: (i // n_seq_tiles, i % n_seq_tiles, 0, 0))
    feat_shape = lambda n: jax.ShapeDtypeStruct((bsz, seq // MLA_TK, n, MLA_TK), BF16)
    return pl.pallas_call(
        _mla_proj_kernel,
        grid=(t // tm,),
        in_specs=[row(d), _const_spec(w_down.shape), _const_spec((1, KV_RANK)), _const_spec((1, Q_RANK)),
                  _const_spec(w_uk.shape), _const_spec(w_uv_t.shape), _const_spec(w_uqn_t.shape),
                  _const_spec(w_uqr_t.shape), tab, tab, tab_t, tab_t],
        out_specs=[feat_t(hn), feat_t(hr), row(hn), row(LANES), feat_t(hv)],
        out_shape=[feat_shape(hn), feat_shape(hr),
                   jax.ShapeDtypeStruct((t, hn), BF16), jax.ShapeDtypeStruct((t, LANES), BF16), feat_shape(hv)],
        compiler_params=_params("parallel"),
        name="mla_proj",
    )(x2d, w_down, kv_g, q_g, w_uk, w_uv_t, w_uqn_t, w_uqr_t, cos, sin, cos_t, sin_t)


MLA_GROUP = 8
MLA_QUNROLL = 4
MLA_TQ = 256
MLA_TK = 256
MLA_ONES = 16


def _mla_attn_kernel(qn_ref, qr_ref, kn_ref, kr_ref, v_ref, o_ref, qt_ref, s0_ref, s1_ref, m_ref, acc_ref,
                     *, n_qblk):
    tq, tk = MLA_TQ, MLA_TK
    heads = range(MLA_GROUP)
    ones = jnp.ones((MLA_ONES, tk), BF16)
    bufs = (s0_ref, s1_ref)

    def build_query(qi, slot):
        rope_slots = LANES // B_ROPE_DIM
        for h in heads:
            zn = jnp.zeros((B_NOPE_DIM, tq), BF16)
            zr = jnp.zeros((B_ROPE_DIM, tq), BF16)
            qn = qn_ref[0, qi, h * B_NOPE_DIM:(h + 1) * B_NOPE_DIM, :]
            qr = qr_ref[0, qi, h * B_ROPE_DIM:(h + 1) * B_ROPE_DIM, :]
            qt_ref[slot, h] = jnp.concatenate([qn, zn][::1 if h % 2 == 0 else -1]
                                              + [qr if g == h % rope_slots else zr for g in range(rope_slots)],
                                              axis=0)

    def scores(j, s_ref, slot):
        ks = pl.multiple_of(j * tk, tk)
        kr = kr_ref[0, pl.ds(ks, tk), :]
        for pair in range(MLA_GROUP // 2):
            k_cat = jnp.concatenate([kn_ref[0, pl.ds(ks, tk), pair * LANES:(pair + 1) * LANES], kr], axis=1)
            for h in (2 * pair, 2 * pair + 1):
                s_ref[h] = jnp.dot(k_cat, qt_ref[slot, h], preferred_element_type=F32)

    def absorb(j, s_ref, masked):
        if masked:
            kl = lax.broadcasted_iota(jnp.int32, (tk, tq), 0)
            ql = lax.broadcasted_iota(jnp.int32, (tk, tq), 1)
            causal = kl <= ql
        mp_all = []
        for h in heads:
            s, m_old = s_ref[h], m_ref[h]
            if masked:
                s = jnp.where(causal, s, NEG)
            m_new = jnp.maximum(m_old, jnp.max(s, axis=0, keepdims=True))
            mp_all.append((m_old, m_new, jnp.exp2(s - m_new).astype(BF16)))
        for h in heads:
            m_old, m_new, p = mp_all[h]
            v1 = jnp.concatenate([v_ref[0, j, h * B_V_DIM:(h + 1) * B_V_DIM, :], ones], axis=0)
            pv = jnp.dot(v1, p, preferred_element_type=F32)
            acc_ref[h] = jnp.exp2(m_old - m_new) * acc_ref[h] + pv
            m_ref[h] = m_new

    def query_block(qi, odd, first, slot):
        cur, other = bufs[first], bufs[1 - first]
        nxt = jnp.minimum(qi + 1, n_qblk - 1)
        m_ref[...] = jnp.full(m_ref.shape, NEG, F32)
        acc_ref[...] = jnp.zeros(acc_ref.shape, F32)

        def two_blocks(i, carry):
            j = 2 * i
            scores(j + 1, other, slot)
            absorb(j, cur, False)
            scores(j + 2, cur, slot)
            absorb(j + 1, other, False)
            return carry

        lax.fori_loop(0, qi // 2, two_blocks, 0)
        build_query(nxt, 1 - slot)
        if odd:
            scores(qi, other, slot)
            absorb(qi - 1, cur, False)
            scores(0, cur, 1 - slot)
            absorb(qi, other, True)
            free = first
        else:
            scores(0, other, 1 - slot)
            absorb(qi, cur, True)
            free = 1 - first

        o_t = jnp.concatenate([acc_ref[h, :B_V_DIM] / acc_ref[h, B_V_DIM:B_V_DIM + 1] for h in heads], axis=0)
        o_ref[0, pl.ds(pl.multiple_of(qi * tq, tq), tq), :] = o_t.T.astype(o_ref.dtype)
        return free

    build_query(0, 0)
    scores(0, bufs[0], 0)

    def query_blocks(u, carry):
        first = 0
        for k in range(MLA_QUNROLL):
            first = query_block(u * MLA_QUNROLL + k, k % 2 == 1, first, k % 2)
        assert first == 0
        return carry

    lax.fori_loop(0, n_qblk // MLA_QUNROLL, query_blocks, 0)


def _mla_attn(qn_t, qr_t, kn, kr, v_t):
    bsz, n_blk, _, _ = qn_t.shape
    seq = n_blk * MLA_TQ
    n_groups = B_HEADS // MLA_GROUP
    gn, gr, gv = MLA_GROUP * B_NOPE_DIM, MLA_GROUP * B_ROPE_DIM, MLA_GROUP * B_V_DIM
    assert MLA_TQ == MLA_TK and MLA_GROUP % 2 == 0 and gn % LANES == 0 and gr % SUBLANES == 0
    assert n_blk % MLA_QUNROLL == 0
    score_buf = pltpu.VMEM((MLA_GROUP, MLA_TK, MLA_TQ), F32)
    blocked = lambda n: pl.BlockSpec((1, n_blk, n, MLA_TK), lambda b, g: (b, 0, g, 0))
    return pl.pallas_call(
        functools.partial(_mla_attn_kernel, n_qblk=n_blk),
        grid=(bsz, n_groups),
        in_specs=[blocked(gn), blocked(gr),
                  pl.BlockSpec((1, seq, gn), lambda b, g: (b, 0, g)),
                  pl.BlockSpec((1, seq, LANES), lambda b, g: (b, 0, 0)),
                  blocked(gv)],
        out_specs=pl.BlockSpec((1, seq, gv), lambda b, g: (b, 0, g)),
        out_shape=jax.ShapeDtypeStruct((bsz, seq, B_HEADS * B_V_DIM), BF16),
        scratch_shapes=[pltpu.VMEM((2, MLA_GROUP, 2 * LANES, MLA_TQ), BF16),
                        score_buf, score_buf,
                        pltpu.VMEM((MLA_GROUP, 1, MLA_TQ), F32),
                        pltpu.VMEM((MLA_GROUP, B_V_DIM + MLA_ONES, MLA_TQ), F32)],
        compiler_params=_params("parallel", "parallel"),
        name="mla_attn",
    )(qn_t, qr_t, kn, kr, v_t)


def _rope_tables(seq):
    inv_freq = ROPE_BASE ** (-jnp.arange(0, B_ROPE_DIM, 2, dtype=F32) / B_ROPE_DIM)
    ang = jnp.arange(seq, dtype=F32)[:, None] * inv_freq[None, :]
    cos, sin = jnp.cos(ang), jnp.sin(ang)
    reps = LANES // B_ROPE_DIM
    cos = jnp.tile(jnp.concatenate([cos, cos], axis=1), (1, reps))
    sin = jnp.tile(jnp.concatenate([-sin, sin], axis=1), (1, reps))
    return cos, sin, cos.T, sin.T


def kernel(x, a_w_qkv, a_w_o, kv_w_dkv, kv_norm_g, kv_w_kr, kv_w_uk, kv_w_uv, b_w_dq, b_q_norm_g, b_w_uq, b_w_o, ffn_w_in, ffn_conv_w, ffn_conv_b, ffn_w_out, ln_mix_g, ln_mix_b, ln_ffn_g, ln_ffn_b):
    bsz, seq, d = x.shape
    t = bsz * seq
    x2d = x.reshape(t, d)
    row = lambda a: a.reshape(1, -1)

    w_in_all, w_out_all = ffn_w_in.astype(BF16), ffn_w_out.astype(BF16)

    def tail(o, xin, w_o, layer):
        return _layer_tail(o.reshape(t, -1), xin, seq, layer, w_o.astype(BF16), row(ln_mix_g[layer]),
                           row(ln_mix_b[layer]), w_in_all, ffn_conv_w[layer], row(ffn_conv_b[layer]),
                           w_out_all, row(ln_ffn_g[layer]), row(ln_ffn_b[layer]))

    slopes = jnp.asarray([2.0 ** (-8.0 * (h + 1) / A_HEADS) for h in range(A_HEADS)], dtype=F32)
    qkv = _qkv_proj(x2d, a_w_qkv[0].astype(BF16))
    o = _dilated_attn(qkv.reshape(bsz, seq, -1), slopes)
    x2d = tail(o, x2d, a_w_o[0], 0)

    cos, sin, cos_t, sin_t = _rope_tables(seq)
    w_down = jnp.concatenate([kv_w_dkv, jnp.tile(kv_w_kr, (1, LANES // B_ROPE_DIM)), b_w_dq[0]], axis=1)
    w_uq = b_w_uq[0].reshape(Q_RANK, B_HEADS, B_NOPE_DIM + B_ROPE_DIM)
    w_uqn_t = w_uq[:, :, :B_NOPE_DIM].reshape(Q_RANK, -1).T
    w_uqr_t = w_uq[:, :, B_NOPE_DIM:].reshape(Q_RANK, -1).T
    qn_t, qr_t, kn, kr, v_t = _mla_proj(x2d, seq, w_down.astype(BF16), row(kv_norm_g), row(b_q_norm_g[0]),
                                        kv_w_uk.astype(BF16), kv_w_uv.T.astype(BF16), w_uqn_t.astype(BF16),
                                        w_uqr_t.astype(BF16), cos, sin, cos_t, sin_t)
    shp = lambda a: a.reshape(bsz, seq, -1)
    o = _mla_attn(qn_t, qr_t, shp(kn), shp(kr), v_t)
    x2d = tail(o, x2d, b_w_o[0], 1)
    return x2d.reshape(bsz, seq, d)
```

```python
import functools

import jax
import jax.numpy as jnp
from jax import lax
from jax.experimental import pallas as pl
from jax.experimental.pallas import tpu as pltpu

D_MODEL = 1024
DEPTH = 2
A_HEADS = 16
A_HEAD_DIM = 64
DILATED_PATTERNS = ((128, 1), (512, 4), (2048, 16))
BAND = 128
B_HEADS = 16
B_NOPE_DIM = 64
B_ROPE_DIM = 32
B_V_DIM = 64
Q_RANK = 384
KV_RANK = 256
ROPE_BASE = 10000.0
D_FF = 2816
CONV_WIDTH = 3
ALPHA = (2.0 * DEPTH) ** 0.25
LN_EPS = 1e-5
RMS_EPS = 1e-6

LANES = 128
SUBLANES = 8
NEG = -1e30
LOG2E = 1.4426950408889634
VMEM_LIMIT = 56 * 1024 * 1024

F32 = jnp.float32
BF16 = jnp.bfloat16


def _const_spec(shape):
    nd = len(shape)
    return pl.BlockSpec(shape, lambda *_: (0,) * nd, pipeline_mode=pl.Buffered(1))


def _layer_spec(shape, layer):
    return pl.BlockSpec((None,) + shape, lambda *_: (layer,) + (0,) * len(shape), pipeline_mode=pl.Buffered(1))


def _params(*sem):
    return pltpu.CompilerParams(dimension_semantics=sem, vmem_limit_bytes=VMEM_LIMIT)


def _layer_norm(y, g, b):
    mu = jnp.mean(y, axis=-1, keepdims=True)
    yc = y - mu
    var = jnp.mean(yc * yc, axis=-1, keepdims=True)
    return yc * lax.rsqrt(var + LN_EPS) * g + b


def _qkv_kernel(x_ref, w_ref, o_ref):
    o_ref[...] = jnp.dot(x_ref[...].astype(BF16), w_ref[...], preferred_element_type=F32)


def _qkv_proj(x2d, w):
    t, d = x2d.shape
    n = w.shape[1]
    tm = 512
    return pl.pallas_call(
        _qkv_kernel,
        grid=(t // tm,),
        in_specs=[pl.BlockSpec((tm, d), lambda i: (i, 0)), _const_spec((d, n))],
        out_specs=pl.BlockSpec((tm, n), lambda i: (i, 0)),
        out_shape=jax.ShapeDtypeStruct((t, n), F32),
        compiler_params=_params("parallel"),
        name="qkv_proj",
    )(x2d, w)


DIL_GROUP_FIRST = 8
DIL_GROUP_RESIDUES = 2
DIL_GROUP_LATER = 8


def _dilated_kernel(slopes_ref, q_ref, k_ref, v_ref, o_ref, ob_ref, lse_ref, bias_ref, *, seq):
    hp = pl.program_id(1)
    lane = lax.broadcasted_iota(jnp.int32, (1, LANES), 1)
    head0 = lane < A_HEAD_DIM
    head_masks = (head0, jnp.logical_not(head0))
    q_scale = A_HEAD_DIM ** -0.5 * LOG2E
    ones = jnp.ones((1, LANES), BF16)

    def fill_bias(dilation):
        for off, width in ((0, BAND), (BAND, 2 * BAND)):
            ql = lax.broadcasted_iota(jnp.int32, (BAND, width), 0)
            kl = lax.broadcasted_iota(jnp.int32, (BAND, width), 1)
            back = ql - kl + (width - BAND)
            valid = (back >= 0) & (back <= BAND)
            dist = (back * dilation).astype(F32)
            for hh in range(2):
                slope = slopes_ref[2 * hp + hh] * LOG2E
                bias_ref[hh, :, off:off + width] = jnp.where(valid, -slope * dist, NEG)

    def load_unit(dilation, q_start, k_start, width):
        q = (q_ref[0, pl.ds(q_start, BAND, stride=dilation), :] * q_scale).astype(BF16)
        k = k_ref[0, pl.ds(k_start, width, stride=dilation), :].astype(BF16)
        v = v_ref[0, pl.ds(k_start, width, stride=dilation), :].astype(BF16)
        return q, k, v

    def scores(q, k):
        width = k.shape[0]
        boff = 0 if width == BAND else BAND
        qs = jnp.concatenate([jnp.where(mask, q, jnp.zeros_like(q)) for mask in head_masks], axis=0)
        s = lax.dot_general(qs, k, (((1,), (1,)), ((), ())), preferred_element_type=F32)
        return [s[hh * BAND:(hh + 1) * BAND] + bias_ref[hh, :, boff:boff + width] for hh in range(2)]

    def softmax(s):
        m = jnp.max(s, axis=-1, keepdims=True)
        return m, jnp.exp2(s - m).astype(BF16)

    def weighted(p, v, hh):
        return jnp.dot(p, jnp.where(head_masks[hh], v, ones), preferred_element_type=F32)

    def run_group(branch, dilation, units):
        loaded = [load_unit(dilation, *u) for u in units]
        s_all = [scores(q, k) for q, k, _ in loaded]
        mp_all = [[softmax(s) for s in s_u] for s_u in s_all]
        pv_all = [[weighted(mp[hh][1], v, hh) for hh in range(2)] for mp, (_, _, v) in zip(mp_all, loaded)]
        for (q_start, _, _), mp, pv in zip(units, mp_all, pv_all):
            num = jnp.where(head0, pv[0], pv[1])
            den = pltpu.roll(jnp.where(head0, pv[1], pv[0]), A_HEAD_DIM, 1)
            rows = pl.ds(q_start, BAND, stride=dilation)
            ob_ref[branch, rows, :] = num / den
            lse_ref[branch, rows, :] = jnp.where(head0, mp[0][0], mp[1][0]) + jnp.log2(den)

    for branch, (_, dilation) in enumerate(DILATED_PATTERNS):
        n_blk = seq // dilation // BAND
        blk_stride = BAND * dilation
        fill_bias(dilation)
        first = lambda r: (r, r, BAND)
        later = lambda r, n, blk_stride=blk_stride: (n * blk_stride + r, (n - 1) * blk_stride + r, 2 * BAND)

        if n_blk == 1:
            group = DIL_GROUP_FIRST
            def body(g, c, branch=branch, dilation=dilation, group=group):
                run_group(branch, dilation, [first(g * group + u) for u in range(group)])
                return c
            lax.fori_loop(0, dilation // group, body, 0)
        elif dilation > 1:
            res = DIL_GROUP_RESIDUES
            def body(g, c, branch=branch, dilation=dilation, n_blk=n_blk, res=res):
                units = []
                for u in range(res):
                    r = g * res + u
                    units += [first(r)] + [later(r, n) for n in range(1, n_blk)]
                run_group(branch, dilation, units)
                return c
            lax.fori_loop(0, dilation // res, body, 0)
        else:
            per = DIL_GROUP_LATER
            assert (n_blk - per) % per == 0
            run_group(branch, dilation, [first(0)] + [later(0, n) for n in range(1, per)])

            def body(i, c, branch=branch, dilation=dilation, per=per):
                run_group(branch, dilation, [later(0, per * (i + 1) + u) for u in range(per)])
                return c
            lax.fori_loop(0, (n_blk - per) // per, body, 0)

    rows_per = 256

    def merge(i, carry):
        rows = pl.ds(pl.multiple_of(i * rows_per, rows_per), rows_per)
        l0, l1, l2 = lse_ref[0, rows, :], lse_ref[1, rows, :], lse_ref[2, rows, :]
        mx = jnp.maximum(jnp.maximum(l0, l1), l2)
        w0, w1, w2 = jnp.exp2(l0 - mx), jnp.exp2(l1 - mx), jnp.exp2(l2 - mx)
        num = w0 * ob_ref[0, rows, :] + w1 * ob_ref[1, rows, :] + w2 * ob_ref[2, rows, :]
        o_ref[0, rows, :] = (num / (w0 + w1 + w2)).astype(o_ref.dtype)
        return carry

    lax.fori_loop(0, seq // rows_per, merge, 0)


def _dilated_attn(qkv, slopes):
    bsz, seq, _ = qkv.shape
    n_pairs = A_HEADS * A_HEAD_DIM // LANES
    assert all(w // d == BAND and seq % (d * BAND) == 0 for w, d in DILATED_PATTERNS)

    def col_spec(part):
        return pl.BlockSpec((1, seq, LANES), lambda b, h, part=part: (b, 0, part * n_pairs + h))

    return pl.pallas_call(
        functools.partial(_dilated_kernel, seq=seq),
        grid=(bsz, n_pairs),
        in_specs=[pl.BlockSpec(memory_space=pltpu.SMEM), col_spec(0), col_spec(1), col_spec(2)],
        out_specs=pl.BlockSpec((1, seq, LANES), lambda b, h: (b, 0, h)),
        out_shape=jax.ShapeDtypeStruct((bsz, seq, A_HEADS * A_HEAD_DIM), BF16),
        scratch_shapes=[pltpu.VMEM((3, seq, LANES), F32), pltpu.VMEM((3, seq, LANES), F32),
                        pltpu.VMEM((2, BAND, 3 * BAND), F32)],
        compiler_params=_params("parallel", "parallel"),
        name="dilated_attn",
    )(slopes, qkv, qkv, qkv)


FFN_COLS = 256


def _tail_kernel(o_ref, x_ref, wo_ref, gm_ref, bm_ref, win_ref, cw_ref, cb_ref, wout_ref, gf_ref, bf_ref,
                 y_ref, u_ref, h_ref, *, tm, tiles_per_seq):
    i = pl.program_id(0)
    halo = SUBLANES

    @pl.when(i % tiles_per_seq == 0)
    def _():
        u_ref[0:halo, :] = jnp.zeros((halo, 2 * D_FF), F32)

    halves = (slice(0, tm // 2), slice(tm // 2, tm))
    mixes = [jnp.dot(o_ref[rows, :], wo_ref[...], preferred_element_type=F32) for rows in halves]
    xs = []
    for rows, mix in zip(halves, mixes):
        x = _layer_norm(ALPHA * x_ref[rows, :] + mix, gm_ref[...], bm_ref[...])
        u_ref[halo + rows.start:halo + rows.stop, :] = jnp.dot(x.astype(BF16), win_ref[...],
                                                               preferred_element_type=F32)
        xs.append(x)

    for c in range(D_FF // FFN_COLS):
        def conv(col):
            cols = slice(col, col + FFN_COLS)
            acc = cb_ref[:, cols] + cw_ref[2:3, cols] * u_ref[halo:halo + tm, cols]
            acc = acc + cw_ref[1:2, cols] * u_ref[halo - 1:halo - 1 + tm, cols]
            return acc + cw_ref[0:1, cols] * u_ref[halo - 2:halo - 2 + tm, cols]

        gate = conv(c * FFN_COLS)
        val = conv(D_FF + c * FFN_COLS)
        h_ref[:, c * FFN_COLS:(c + 1) * FFN_COLS] = (jax.nn.silu(gate) * val).astype(BF16)

    u_ref[0:halo, :] = u_ref[tm:tm + halo, :]

    fs = [jnp.dot(h_ref[rows, :], wout_ref[...], preferred_element_type=F32) for rows in halves]
    for rows, x, f in zip(halves, xs, fs):
        y_ref[rows, :] = _layer_norm(ALPHA * x + f, gf_ref[...], bf_ref[...])


def _layer_tail(o2d, x2d, seq, layer, w_o, g_mix, b_mix, w_in, conv_w, conv_b, w_out, g_ffn, b_ffn):
    t, d = x2d.shape
    k = o2d.shape[1]
    tm = 512
    assert seq % tm == 0 and D_FF % FFN_COLS == 0 and CONV_WIDTH - 1 <= SUBLANES
    return pl.pallas_call(
        functools.partial(_tail_kernel, tm=tm, tiles_per_seq=seq // tm),
        grid=(t // tm,),
        in_specs=[pl.BlockSpec((tm, k), lambda i: (i, 0)), pl.BlockSpec((tm, d), lambda i: (i, 0)),
                  _const_spec((k, d)), _const_spec((1, d)), _const_spec((1, d)),
                  _layer_spec((d, 2 * D_FF), layer), _const_spec((CONV_WIDTH, 2 * D_FF)), _const_spec((1, 2 * D_FF)),
                  _layer_spec((D_FF, d), layer), _const_spec((1, d)), _const_spec((1, d))],
        out_specs=pl.BlockSpec((tm, d), lambda i: (i, 0)),
        out_shape=jax.ShapeDtypeStruct((t, d), F32),
        scratch_shapes=[pltpu.VMEM((tm + SUBLANES, 2 * D_FF), F32), pltpu.VMEM((tm, D_FF), BF16)],
        compiler_params=_params("arbitrary"),
        name="layer_tail",
    )(o2d, x2d, w_o, g_mix, b_mix, w_in, conv_w, conv_b, w_out, g_ffn, b_ffn)


def _rope(t, cos, sin_signed):
    lane = lax.broadcasted_iota(jnp.int32, t.shape, 1)
    half = B_ROPE_DIM // 2
    first = (lane % B_ROPE_DIM) < half
    swapped = jnp.where(first, pltpu.roll(t, LANES - half, 1), pltpu.roll(t, half, 1))
    return t * cos + swapped * sin_signed


def _rope_t(t, cos, sin_signed):
    half = B_ROPE_DIM // 2
    pieces = [t[r * half:(r + 1) * half] for r in range(t.shape[0] // half)]
    swapped = jnp.concatenate([pieces[r ^ 1] for r in range(len(pieces))], axis=0)
    return t * cos + swapped * sin_signed


def _rms(c, g):
    return c * lax.rsqrt(jnp.mean(c * c, axis=-1, keepdims=True) + RMS_EPS) * g


def _mla_proj_kernel(x_ref, wd_ref, kvg_ref, qg_ref, wuk_ref, wuvt_ref, wuqnt_ref, wuqrt_ref,
                     cos_ref, sin_ref, cost_ref, sint_ref, qn_ref, qr_ref, kn_ref, kr_ref, v_ref):
    nt = (((1,), (1,)), ((), ()))
    x = x_ref[...].astype(BF16)
    down = jnp.dot(x, wd_ref[...], preferred_element_type=F32)

    c_kv = _rms(down[:, :KV_RANK], kvg_ref[...]).astype(BF16)
    kn_ref[...] = jnp.dot(c_kv, wuk_ref[...], preferred_element_type=F32).astype(BF16)
    kr_ref[...] = _rope(down[:, KV_RANK:KV_RANK + LANES], cos_ref[...], sin_ref[...]).astype(BF16)
    v_t = lax.dot_general(wuvt_ref[...], c_kv, nt, preferred_element_type=F32)
    for c in range(v_t.shape[1] // MLA_TK):
        v_ref[0, c] = v_t[:, c * MLA_TK:(c + 1) * MLA_TK].astype(BF16)

    scale = (B_NOPE_DIM + B_ROPE_DIM) ** -0.5 * LOG2E
    c_q = _rms(down[:, KV_RANK + LANES:], qg_ref[...]).astype(BF16)
    qn_t = (lax.dot_general(wuqnt_ref[...], c_q, nt, preferred_element_type=F32) * scale).astype(BF16)
    qr_t = lax.dot_general(wuqrt_ref[...], c_q, nt, preferred_element_type=F32)
    cos_t, sin_t = cost_ref[...], sint_ref[...]
    qr_t = jnp.concatenate([(_rope_t(qr_t[c * LANES:(c + 1) * LANES], cos_t, sin_t) * scale).astype(BF16)
                            for c in range(qr_t.shape[0] // LANES)], axis=0)
    for c in range(qn_t.shape[1] // MLA_TQ):
        cols = slice(c * MLA_TQ, (c + 1) * MLA_TQ)
        qn_ref[0, c] = qn_t[:, cols]
        qr_ref[0, c] = qr_t[:, cols]


def _mla_proj(x2d, seq, w_down, kv_g, q_g, w_uk, w_uv_t, w_uqn_t, w_uqr_t, cos, sin, cos_t, sin_t):
    t, d = x2d.shape
    tm = 512
    n_seq_tiles = seq // tm
    bsz = t // seq
    blk = tm // MLA_TK
    hn, hr, hv = B_HEADS * B_NOPE_DIM, B_HEADS * B_ROPE_DIM, B_HEADS * B_V_DIM
    row = lambda n: pl.BlockSpec((tm, n), lambda i: (i, 0))
    tab = pl.BlockSpec((tm, LANES), lambda i: (i % n_seq_tiles, 0))
    tab_t = pl.BlockSpec((LANES, tm), lambda i: (0, i % n_seq_tiles))
    feat_t = lambda n: pl.BlockSpec((1, blk, n, MLA_TK), lambda i: (i // n_seq_tiles, i % n_seq_tiles, 0, 0))
    feat_shape = lambda n: jax.ShapeDtypeStruct((bsz, seq // MLA_TK, n, MLA_TK), BF16)
    return pl.pallas_call(
        _mla_proj_kernel,
        grid=(t // tm,),
        in_specs=[row(d), _const_spec(w_down.shape), _const_spec((1, KV_RANK)), _const_spec((1, Q_RANK)),
                  _const_spec(w_uk.shape), _const_spec(w_uv_t.shape), _const_spec(w_uqn_t.shape),
                  _const_spec(w_uqr_t.shape), tab, tab, tab_t, tab_t],
        out_specs=[feat_t(hn), feat_t(hr), row(hn), row(LANES), feat_t(hv)],
        out_shape=[feat_shape(hn), feat_shape(hr),
                   jax.ShapeDtypeStruct((t, hn), BF16), jax.ShapeDtypeStruct((t, LANES), BF16), feat_shape(hv)],
        compiler_params=_params("parallel"),
        name="mla_proj",
    )(x2d, w_down, kv_g, q_g, w_uk, w_uv_t, w_uqn_t, w_uqr_t, cos, sin, cos_t, sin_t)


MLA_GROUP = 8
MLA_QUNROLL = 4
MLA_TQ = 256
MLA_TK = 256
MLA_ONES = 16


def _mla_attn_kernel(qn_ref, qr_ref, kn_ref, kr_ref, v_ref, o_ref, qt_ref, s0_ref, s1_ref, m_ref, acc_ref,
                     *, n_qblk):
    tq, tk = MLA_TQ, MLA_TK
    heads = range(MLA_GROUP)
    ones = jnp.ones((MLA_ONES, tk), BF16)
    bufs = (s0_ref, s1_ref)

    def build_query(qi, slot):
        rope_slots = LANES // B_ROPE_DIM
        for h in heads:
            zn = jnp.zeros((B_NOPE_DIM, tq), BF16)
            zr = jnp.zeros((B_ROPE_DIM, tq), BF16)
            qn = qn_ref[0, qi, h * B_NOPE_DIM:(h + 1) * B_NOPE_DIM, :]
            qr = qr_ref[0, qi, h * B_ROPE_DIM:(h + 1) * B_ROPE_DIM, :]
            qt_ref[slot, h] = jnp.concatenate([qn, zn][::1 if h % 2 == 0 else -1]
                                              + [qr if g == h % rope_slots else zr for g in range(rope_slots)],
                                              axis=0)

    def scores(j, s_ref, slot):
        ks = pl.multiple_of(j * tk, tk)
        kr = kr_ref[0, pl.ds(ks, tk), :]
        for pair in range(MLA_GROUP // 2):
            k_cat = jnp.concatenate([kn_ref[0, pl.ds(ks, tk), pair * LANES:(pair + 1) * LANES], kr], axis=1)
            for h in (2 * pair, 2 * pair + 1):
                s_ref[h] = jnp.dot(k_cat, qt_ref[slot, h], preferred_element_type=F32)

    def absorb(j, s_ref, masked):
        if masked:
            kl = lax.broadcasted_iota(jnp.int32, (tk, tq), 0)
            ql = lax.broadcasted_iota(jnp.int32, (tk, tq), 1)
            causal = kl <= ql
        mp_all = []
        for h in heads:
            s, m_old = s_ref[h], m_ref[h]
            if masked:
                s = jnp.where(causal, s, NEG)
            m_new = jnp.maximum(m_old, jnp.max(s, axis=0, keepdims=True))
            mp_all.append((m_old, m_new, jnp.exp2(s - m_new).astype(BF16)))
        for h in heads:
            m_old, m_new, p = mp_all[h]
            v1 = jnp.concatenate([v_ref[0, j, h * B_V_DIM:(h + 1) * B_V_DIM, :], ones], axis=0)
            pv = jnp.dot(v1, p, preferred_element_type=F32)
            acc_ref[h] = jnp.exp2(m_old - m_new) * acc_ref[h] + pv
            m_ref[h] = m_new

    def query_block(qi, odd, first, slot):
        cur, other = bufs[first], bufs[1 - first]
        nxt = jnp.minimum(qi + 1, n_qblk - 1)
        m_ref[...] = jnp.full(m_ref.shape, NEG, F32)
        acc_ref[...] = jnp.zeros(acc_ref.shape, F32)

        def two_blocks(i, carry):
            j = 2 * i
            scores(j + 1, other, slot)
            absorb(j, cur, False)
            scores(j + 2, cur, slot)
            absorb(j + 1, other, False)
            return carry

        lax.fori_loop(0, qi // 2, two_blocks, 0)
        build_query(nxt, 1 - slot)
        if odd:
            scores(qi, other, slot)
            absorb(qi - 1, cur, False)
            scores(0, cur, 1 - slot)
            absorb(qi, other, True)
            free = first
        else:
            scores(0, other, 1 - slot)
            absorb(qi, cur, True)
            free = 1 - first

        o_t = jnp.concatenate([acc_ref[h, :B_V_DIM] / acc_ref[h, B_V_DIM:B_V_DIM + 1] for h in heads], axis=0)
        o_ref[0, pl.ds(pl.multiple_of(qi * tq, tq), tq), :] = o_t.T.astype(o_ref.dtype)
        return free

    build_query(0, 0)
    scores(0, bufs[0], 0)

    def query_blocks(u, carry):
        first = 0
        for k in range(MLA_QUNROLL):
            first = query_block(u * MLA_QUNROLL + k, k % 2 == 1, first, k % 2)
        assert first == 0
        return carry

    lax.fori_loop(0, n_qblk // MLA_QUNROLL, query_blocks, 0)


def _mla_attn(qn_t, qr_t, kn, kr, v_t):
    bsz, n_blk, _, _ = qn_t.shape
    seq = n_blk * MLA_TQ
    n_groups = B_HEADS // MLA_GROUP
    gn, gr, gv = MLA_GROUP * B_NOPE_DIM, MLA_GROUP * B_ROPE_DIM, MLA_GROUP * B_V_DIM
    assert MLA_TQ == MLA_TK and MLA_GROUP % 2 == 0 and gn % LANES == 0 and gr % SUBLANES == 0
    assert n_blk % MLA_QUNROLL == 0
    score_buf = pltpu.VMEM((MLA_GROUP, MLA_TK, MLA_TQ), F32)
    blocked = lambda n: pl.BlockSpec((1, n_blk, n, MLA_TK), lambda b, g: (b, 0, g, 0))
    return pl.pallas_call(
        functools.partial(_mla_attn_kernel, n_qblk=n_blk),
        grid=(bsz, n_groups),
        in_specs=[blocked(gn), blocked(gr),
                  pl.BlockSpec((1, seq, gn), lambda b, g: (b, 0, g)),
                  pl.BlockSpec((1, seq, LANES), lambda b, g: (b, 0, 0)),
                  blocked(gv)],
        out_specs=pl.BlockSpec((1, seq, gv), lambda b, g: (b, 0, g)),
        out_shape=jax.ShapeDtypeStruct((bsz, seq, B_HEADS * B_V_DIM), BF16),
        scratch_shapes=[pltpu.VMEM((2, MLA_GROUP, 2 * LANES, MLA_TQ), BF16),
                        score_buf, score_buf,
                        pltpu.VMEM((MLA_GROUP, 1, MLA_TQ), F32),
                        pltpu.VMEM((MLA_GROUP, B_V_DIM + MLA_ONES, MLA_TQ), F32)],
        compiler_params=_params("parallel", "parallel"),
        name="mla_attn",
    )(qn_t, qr_t, kn, kr, v_t)


def _rope_tables(seq):
    inv_freq = ROPE_BASE ** (-jnp.arange(0, B_ROPE_DIM, 2, dtype=F32) / B_ROPE_DIM)
    ang = jnp.arange(seq, dtype=F32)[:, None] * inv_freq[None, :]
    cos, sin = jnp.cos(ang), jnp.sin(ang)
    reps = LANES // B_ROPE_DIM
    cos = jnp.tile(jnp.concatenate([cos, cos], axis=1), (1, reps))
    sin = jnp.tile(jnp.concatenate([-sin, sin], axis=1), (1, reps))
    return cos, sin, cos.T, sin.T


def kernel(x, a_w_qkv, a_w_o, kv_w_dkv, kv_norm_g, kv_w_kr, kv_w_uk, kv_w_uv, b_w_dq, b_q_norm_g, b_w_uq, b_w_o, ffn_w_in, ffn_conv_w, ffn_conv_b, ffn_w_out, ln_mix_g, ln_mix_b, ln_ffn_g, ln_ffn_b):
    bsz, seq, d = x.shape
    t = bsz * seq
    x2d = x.reshape(t, d)
    row = lambda a: a.reshape(1, -1)

    w_in_all, w_out_all = ffn_w_in.astype(BF16), ffn_w_out.astype(BF16)

    def tail(o, xin, w_o, layer):
        return _layer_tail(o.reshape(t, -1), xin, seq, layer, w_o.astype(BF16), row(ln_mix_g[layer]),
                           row(ln_mix_b[layer]), w_in_all, ffn_conv_w[layer], row(ffn_conv_b[layer]),
                           w_out_all, row(ln_ffn_g[layer]), row(ln_ffn_b[layer]))

    slopes = jnp.asarray([2.0 ** (-8.0 * (h + 1) / A_HEADS) for h in range(A_HEADS)], dtype=F32)
    qkv = _qkv_proj(x2d, a_w_qkv[0].astype(BF16))
    o = _dilated_attn(qkv.reshape(bsz, seq, -1), slopes)
    x2d = tail(o, x2d, a_w_o[0], 0)

    cos, sin, cos_t, sin_t = _rope_tables(seq)
    w_down = jnp.concatenate([kv_w_dkv, jnp.tile(kv_w_kr, (1, LANES // B_ROPE_DIM)), b_w_dq[0]], axis=1)
    w_uq = b_w_uq[0].reshape(Q_RANK, B_HEADS, B_NOPE_DIM + B_ROPE_DIM)
    w_uqn_t = w_uq[:, :, :B_NOPE_DIM].reshape(Q_RANK, -1).T
    w_uqr_t = w_uq[:, :, B_NOPE_DIM:].reshape(Q_RANK, -1).T
    qn_t, qr_t, kn, kr, v_t = _mla_proj(x2d, seq, w_down.astype(BF16), row(kv_norm_g), row(b_q_norm_g[0]),
                                        kv_w_uk.astype(BF16), kv_w_uv.T.astype(BF16), w_uqn_t.astype(BF16),
                                        w_uqr_t.astype(BF16), cos, sin, cos_t, sin_t)
    shp = lambda a: a.reshape(bsz, seq, -1)
    o = _mla_attn(qn_t, qr_t, shp(kn), shp(kr), v_t)
    x2d = tail(o, x2d, b_w_o[0], 1)
    return x2d.reshape(bsz, seq, d)
```

```python
import functools

import jax
import jax.numpy as jnp
from jax import lax
from jax.experimental import pallas as pl
from jax.experimental.pallas import tpu as pltpu

D_MODEL = 1024
DEPTH = 2
A_HEADS = 16
A_HEAD_DIM = 64
DILATED_PATTERNS = ((128, 1), (512, 4), (2048, 16))
BAND = 128
B_HEADS = 16
B_NOPE_DIM = 64
B_ROPE_DIM = 32
B_V_DIM = 64
Q_RANK = 384
KV_RANK = 256
ROPE_BASE = 10000.0
D_FF = 2816
CONV_WIDTH = 3
ALPHA = (2.0 * DEPTH) ** 0.25
LN_EPS = 1e-5
RMS_EPS = 1e-6

LANES = 128
SUBLANES = 8
NEG = -1e30
LOG2E = 1.4426950408889634
VMEM_LIMIT = 56 * 1024 * 1024

F32 = jnp.float32
BF16 = jnp.bfloat16


def _const_spec(shape):
    nd = len(shape)
    return pl.BlockSpec(shape, lambda *_: (0,) * nd, pipeline_mode=pl.Buffered(1))


def _layer_spec(shape, layer):
    return pl.BlockSpec((None,) + shape, lambda *_: (layer,) + (0,) * len(shape), pipeline_mode=pl.Buffered(1))


def _params(*sem):
    return pltpu.CompilerParams(dimension_semantics=sem, vmem_limit_bytes=VMEM_LIMIT)


def _layer_norm(y, g, b):
    mu = jnp.mean(y, axis=-1, keepdims=True)
    yc = y - mu
    var = jnp.mean(yc * yc, axis=-1, keepdims=True)
    return yc * lax.rsqrt(var + LN_EPS) * g + b


def _qkv_kernel(x_ref, w_ref, o_ref, wb_ref):
    @pl.when(pl.program_id(0) == 0)
    def _():
        wb_ref[...] = w_ref[...].astype(BF16)

    o_ref[...] = jnp.dot(x_ref[...].astype(BF16), wb_ref[...], preferred_element_type=F32)


def _qkv_proj(x2d, w):
    t, d = x2d.shape
    n = w.shape[1]
    tm = 512
    return pl.pallas_call(
        _qkv_kernel,
        grid=(t // tm,),
        in_specs=[pl.BlockSpec((tm, d), lambda i: (i, 0)), _const_spec((d, n))],
        out_specs=pl.BlockSpec((tm, n), lambda i: (i, 0)),
        out_shape=jax.ShapeDtypeStruct((t, n), F32),
        scratch_shapes=[pltpu.VMEM((d, n), BF16)],
        compiler_params=_params("arbitrary"),
        name="qkv_proj",
    )(x2d, w)


DIL_GROUP_FIRST = 8
DIL_GROUP_RESIDUES = 2
DIL_GROUP_LATER = 8


def _dilated_kernel(slopes_ref, q_ref, k_ref, v_ref, o_ref, ob_ref, lse_ref, bias_ref, *, seq):
    hp = pl.program_id(1)
    lane = lax.broadcasted_iota(jnp.int32, (1, LANES), 1)
    head0 = lane < A_HEAD_DIM
    head_masks = (head0, jnp.logical_not(head0))
    q_scale = A_HEAD_DIM ** -0.5 * LOG2E
    ones = jnp.ones((1, LANES), BF16)

    def fill_bias(dilation):
        for off, width in ((0, BAND), (BAND, 2 * BAND)):
            ql = lax.broadcasted_iota(jnp.int32, (BAND, width), 0)
            kl = lax.broadcasted_iota(jnp.int32, (BAND, width), 1)
            back = ql - kl + (width - BAND)
            valid = (back >= 0) & (back <= BAND)
            dist = (back * dilation).astype(F32)
            for hh in range(2):
                slope = slopes_ref[2 * hp + hh] * LOG2E
                bias_ref[hh, :, off:off + width] = jnp.where(valid, -slope * dist, NEG)

    def load_unit(dilation, q_start, k_start, width):
        q = (q_ref[0, pl.ds(q_start, BAND, stride=dilation), :] * q_scale).astype(BF16)
        k = k_ref[0, pl.ds(k_start, width, stride=dilation), :].astype(BF16)
        v = v_ref[0, pl.ds(k_start, width, stride=dilation), :].astype(BF16)
        return q, k, v

    def scores(q, k):
        width = k.shape[0]
        boff = 0 if width == BAND else BAND
        qs = jnp.concatenate([jnp.where(mask, q, jnp.zeros_like(q)) for mask in head_masks], axis=0)
        s = lax.dot_general(qs, k, (((1,), (1,)), ((), ())), preferred_element_type=F32)
        return [s[hh * BAND:(hh + 1) * BAND] + bias_ref[hh, :, boff:boff + width] for hh in range(2)]

    def softmax(s):
        m = jnp.max(s, axis=-1, keepdims=True)
        return m, jnp.exp2(s - m).astype(BF16)

    def weighted(p, v, hh):
        return jnp.dot(p, jnp.where(head_masks[hh], v, ones), preferred_element_type=F32)

    def run_group(branch, dilation, units):
        loaded = [load_unit(dilation, *u) for u in units]
        s_all = [scores(q, k) for q, k, _ in loaded]
        mp_all = [[softmax(s) for s in s_u] for s_u in s_all]
        pv_all = [[weighted(mp[hh][1], v, hh) for hh in range(2)] for mp, (_, _, v) in zip(mp_all, loaded)]
        for (q_start, _, _), mp, pv in zip(units, mp_all, pv_all):
            num = jnp.where(head0, pv[0], pv[1])
            den = pltpu.roll(jnp.where(head0, pv[1], pv[0]), A_HEAD_DIM, 1)
            rows = pl.ds(q_start, BAND, stride=dilation)
            ob_ref[branch, rows, :] = num / den
            lse_ref[branch, rows, :] = jnp.where(head0, mp[0][0], mp[1][0]) + jnp.log2(den)

    for branch, (_, dilation) in enumerate(DILATED_PATTERNS):
        n_blk = seq // dilation // BAND
        blk_stride = BAND * dilation
        fill_bias(dilation)
        first = lambda r: (r, r, BAND)
        later = lambda r, n, blk_stride=blk_stride: (n * blk_stride + r, (n - 1) * blk_stride + r, 2 * BAND)

        if n_blk == 1:
            group = DIL_GROUP_FIRST
            def body(g, c, branch=branch, dilation=dilation, group=group):
                run_group(branch, dilation, [first(g * group + u) for u in range(group)])
                return c
            lax.fori_loop(0, dilation // group, body, 0)
        elif dilation > 1:
            res = DIL_GROUP_RESIDUES
            def body(g, c, branch=branch, dilation=dilation, n_blk=n_blk, res=res):
                units = []
                for u in range(res):
                    r = g * res + u
                    units += [first(r)] + [later(r, n) for n in range(1, n_blk)]
                run_group(branch, dilation, units)
                return c
            lax.fori_loop(0, dilation // res, body, 0)
        else:
            per = DIL_GROUP_LATER
            assert (n_blk - per) % per == 0
            run_group(branch, dilation, [first(0)] + [later(0, n) for n in range(1, per)])

            def body(i, c, branch=branch, dilation=dilation, per=per):
                run_group(branch, dilation, [later(0, per * (i + 1) + u) for u in range(per)])
                return c
            lax.fori_loop(0, (n_blk - per) // per, body, 0)

    rows_per = 256

    def merge(i, carry):
        rows = pl.ds(pl.multiple_of(i * rows_per, rows_per), rows_per)
        l0, l1, l2 = lse_ref[0, rows, :], lse_ref[1, rows, :], lse_ref[2, rows, :]
        mx = jnp.maximum(jnp.maximum(l0, l1), l2)
        w0, w1, w2 = jnp.exp2(l0 - mx), jnp.exp2(l1 - mx), jnp.exp2(l2 - mx)
        num = w0 * ob_ref[0, rows, :] + w1 * ob_ref[1, rows, :] + w2 * ob_ref[2, rows, :]
        o_ref[0, rows, :] = (num / (w0 + w1 + w2)).astype(o_ref.dtype)
        return carry

    lax.fori_loop(0, seq // rows_per, merge, 0)


def _dilated_attn(qkv, slopes):
    bsz, seq, _ = qkv.shape
    n_pairs = A_HEADS * A_HEAD_DIM // LANES
    assert all(w // d == BAND and seq % (d * BAND) == 0 for w, d in DILATED_PATTERNS)

    def col_spec(part):
        return pl.BlockSpec((1, seq, LANES), lambda b, h, part=part: (b, 0, part * n_pairs + h))

    return pl.pallas_call(
        functools.partial(_dilated_kernel, seq=seq),
        grid=(bsz, n_pairs),
        in_specs=[pl.BlockSpec(memory_space=pltpu.SMEM), col_spec(0), col_spec(1), col_spec(2)],
        out_specs=pl.BlockSpec((1, seq, LANES), lambda b, h: (b, 0, h)),
        out_shape=jax.ShapeDtypeStruct((bsz, seq, A_HEADS * A_HEAD_DIM), BF16),
        scratch_shapes=[pltpu.VMEM((3, seq, LANES), F32), pltpu.VMEM((3, seq, LANES), F32),
                        pltpu.VMEM((2, BAND, 3 * BAND), F32)],
        compiler_params=_params("parallel", "parallel"),
        name="dilated_attn",
    )(slopes, qkv, qkv, qkv)


FFN_COLS = 256


def _tail_kernel(o_ref, x_ref, wo_ref, gm_ref, bm_ref, win_ref, cw_ref, cb_ref, wout_ref, gf_ref, bf_ref,
                 y_ref, u_ref, h_ref, *, tm, tiles_per_seq):
    i = pl.program_id(0)
    halo = SUBLANES

    @pl.when(i % tiles_per_seq == 0)
    def _():
        u_ref[0:halo, :] = jnp.zeros((halo, 2 * D_FF), F32)

    halves = (slice(0, tm // 2), slice(tm // 2, tm))
    mixes = [jnp.dot(o_ref[rows, :], wo_ref[...], preferred_element_type=F32) for rows in halves]
    xs = []
    for rows, mix in zip(halves, mixes):
        x = _layer_norm(ALPHA * x_ref[rows, :] + mix, gm_ref[...], bm_ref[...])
        u_ref[halo + rows.start:halo + rows.stop, :] = jnp.dot(x.astype(BF16), win_ref[...],
                                                               preferred_element_type=F32)
        xs.append(x)

    for c in range(D_FF // FFN_COLS):
        def conv(col):
            cols = slice(col, col + FFN_COLS)
            acc = cb_ref[:, cols] + cw_ref[2:3, cols] * u_ref[halo:halo + tm, cols]
            acc = acc + cw_ref[1:2, cols] * u_ref[halo - 1:halo - 1 + tm, cols]
            return acc + cw_ref[0:1, cols] * u_ref[halo - 2:halo - 2 + tm, cols]

        gate = conv(c * FFN_COLS)
        val = conv(D_FF + c * FFN_COLS)
        h_ref[:, c * FFN_COLS:(c + 1) * FFN_COLS] = (jax.nn.silu(gate) * val).astype(BF16)

    u_ref[0:halo, :] = u_ref[tm:tm + halo, :]

    fs = [jnp.dot(h_ref[rows, :], wout_ref[...], preferred_element_type=F32) for rows in halves]
    for rows, x, f in zip(halves, xs, fs):
        y_ref[rows, :] = _layer_norm(ALPHA * x + f, gf_ref[...], bf_ref[...])


def _layer_tail(o2d, x2d, seq, layer, w_o, g_mix, b_mix, w_in, conv_w, conv_b, w_out, g_ffn, b_ffn):
    t, d = x2d.shape
    k = o2d.shape[1]
    tm = 512
    assert seq % tm == 0 and D_FF % FFN_COLS == 0 and CONV_WIDTH - 1 <= SUBLANES
    return pl.pallas_call(
        functools.partial(_tail_kernel, tm=tm, tiles_per_seq=seq // tm),
        grid=(t // tm,),
        in_specs=[pl.BlockSpec((tm, k), lambda i: (i, 0)), pl.BlockSpec((tm, d), lambda i: (i, 0)),
                  _const_spec((k, d)), _const_spec((1, d)), _const_spec((1, d)),
                  _layer_spec((d, 2 * D_FF), layer), _const_spec((CONV_WIDTH, 2 * D_FF)), _const_spec((1, 2 * D_FF)),
                  _layer_spec((D_FF, d), layer), _const_spec((1, d)), _const_spec((1, d))],
        out_specs=pl.BlockSpec((tm, d), lambda i: (i, 0)),
        out_shape=jax.ShapeDtypeStruct((t, d), F32),
        scratch_shapes=[pltpu.VMEM((tm + SUBLANES, 2 * D_FF), F32), pltpu.VMEM((tm, D_FF), BF16)],
        compiler_params=_params("arbitrary"),
        name="layer_tail",
    )(o2d, x2d, w_o, g_mix, b_mix, w_in, conv_w, conv_b, w_out, g_ffn, b_ffn)


def _rope(t, cos, sin_signed):
    lane = lax.broadcasted_iota(jnp.int32, t.shape, 1)
    half = B_ROPE_DIM // 2
    first = (lane % B_ROPE_DIM) < half
    swapped = jnp.where(first, pltpu.roll(t, LANES - half, 1), pltpu.roll(t, half, 1))
    return t * cos + swapped * sin_signed


def _rope_t(t, cos, sin_signed):
    half = B_ROPE_DIM // 2
    pieces = [t[r * half:(r + 1) * half] for r in range(t.shape[0] // half)]
    swapped = jnp.concatenate([pieces[r ^ 1] for r in range(len(pieces))], axis=0)
    return t * cos + swapped * sin_signed


def _rms(c, g):
    return c * lax.rsqrt(jnp.mean(c * c, axis=-1, keepdims=True) + RMS_EPS) * g


def _mla_proj_kernel(x_ref, wd_ref, kvg_ref, qg_ref, wuk_ref, wuvt_ref, wuqnt_ref, wuqrt_ref,
                     cos_ref, sin_ref, cost_ref, sint_ref, qn_ref, qr_ref, kn_ref, kr_ref, v_ref):
    nt = (((1,), (1,)), ((), ()))
    x = x_ref[...].astype(BF16)
    down = jnp.dot(x, wd_ref[...], preferred_element_type=F32)

    c_kv = _rms(down[:, :KV_RANK], kvg_ref[...]).astype(BF16)
    kn_ref[...] = jnp.dot(c_kv, wuk_ref[...], preferred_element_type=F32).astype(BF16)
    kr_ref[...] = _rope(down[:, KV_RANK:KV_RANK + LANES], cos_ref[...], sin_ref[...]).astype(BF16)
    v_t = lax.dot_general(wuvt_ref[...], c_kv, nt, preferred_element_type=F32)
    for c in range(v_t.shape[1] // MLA_TK):
        v_ref[0, c] = v_t[:, c * MLA_TK:(c + 1) * MLA_TK].astype(BF16)

    scale = (B_NOPE_DIM + B_ROPE_DIM) ** -0.5 * LOG2E
    c_q = _rms(down[:, KV_RANK + LANES:], qg_ref[...]).astype(BF16)
    qn_t = (lax.dot_general(wuqnt_ref[...], c_q, nt, preferred_element_type=F32) * scale).astype(BF16)
    qr_t = lax.dot_general(wuqrt_ref[...], c_q, nt, preferred_element_type=F32)
    cos_t, sin_t = cost_ref[...], sint_ref[...]
    qr_t = jnp.concatenate([(_rope_t(qr_t[c * LANES:(c + 1) * LANES], cos_t, sin_t) * scale).astype(BF16)
                            for c in range(qr_t.shape[0] // LANES)], axis=0)
    for c in range(qn_t.shape[1] // MLA_TQ):
        cols = slice(c * MLA_TQ, (c + 1) * MLA_TQ)
        qn_ref[0, c] = qn_t[:, cols]
        qr_ref[0, c] = qr_t[:, cols]


def _mla_proj(x2d, seq, w_down, kv_g, q_g, w_uk, w_uv_t, w_uqn_t, w_uqr_t, cos, sin, cos_t, sin_t):
    t, d = x2d.shape
    tm = 512
    n_seq_tiles = seq // tm
    bsz = t // seq
    blk = tm // MLA_TK
    hn, hr, hv = B_HEADS * B_NOPE_DIM, B_HEADS * B_ROPE_DIM, B_HEADS * B_V_DIM
    row = lambda n: pl.BlockSpec((tm, n), lambda i: (i, 0))
    tab = pl.BlockSpec((tm, LANES), lambda i: (i % n_seq_tiles, 0))
    tab_t = pl.BlockSpec((LANES, tm), lambda i: (0, i % n_seq_tiles))
    feat_t = lambda n: pl.BlockSpec((1, blk, n, MLA_TK), lambda i: (i // n_seq_tiles, i % n_seq_tiles, 0, 0))
    feat_shape = lambda n: jax.ShapeDtypeStruct((bsz, seq // MLA_TK, n, MLA_TK), BF16)
    return pl.pallas_call(
        _mla_proj_kernel,
        grid=(t // tm,),
        in_specs=[row(d), _const_spec(w_down.shape), _const_spec((1, KV_RANK)), _const_spec((1, Q_RANK)),
                  _const_spec(w_uk.shape), _const_spec(w_uv_t.shape), _const_spec(w_uqn_t.shape),
                  _const_spec(w_uqr_t.shape), tab, tab, tab_t, tab_t],
        out_specs=[feat_t(hn), feat_t(hr), row(hn), row(LANES), feat_t(hv)],
        out_shape=[feat_shape(hn), feat_shape(hr),
                   jax.ShapeDtypeStruct((t, hn), BF16), jax.ShapeDtypeStruct((t, LANES), BF16), feat_shape(hv)],
        compiler_params=_params("parallel"),
        name="mla_proj",
    )(x2d, w_down, kv_g, q_g, w_uk, w_uv_t, w_uqn_t, w_uqr_t, cos, sin, cos_t, sin_t)


MLA_GROUP = 8
MLA_QUNROLL = 4
MLA_TQ = 256
MLA_TK = 256
MLA_ONES = 16


def _mla_attn_kernel(qn_ref, qr_ref, kn_ref, kr_ref, v_ref, o_ref, qt_ref, s0_ref, s1_ref, m_ref, acc_ref,
                     *, n_qblk):
    tq, tk = MLA_TQ, MLA_TK
    heads = range(MLA_GROUP)
    ones = jnp.ones((MLA_ONES, tk), BF16)
    bufs = (s0_ref, s1_ref)

    def build_query(qi, slot):
        rope_slots = LANES // B_ROPE_DIM
        for h in heads:
            zn = jnp.zeros((B_NOPE_DIM, tq), BF16)
            zr = jnp.zeros((B_ROPE_DIM, tq), BF16)
            qn = qn_ref[0, qi, h * B_NOPE_DIM:(h + 1) * B_NOPE_DIM, :]
            qr = qr_ref[0, qi, h * B_ROPE_DIM:(h + 1) * B_ROPE_DIM, :]
            qt_ref[slot, h] = jnp.concatenate([qn, zn][::1 if h % 2 == 0 else -1]
                                              + [qr if g == h % rope_slots else zr for g in range(rope_slots)],
                                              axis=0)

    def scores(j, s_ref, slot):
        ks = pl.multiple_of(j * tk, tk)
        kr = kr_ref[0, pl.ds(ks, tk), :]
        for pair in range(MLA_GROUP // 2):
            k_cat = jnp.concatenate([kn_ref[0, pl.ds(ks, tk), pair * LANES:(pair + 1) * LANES], kr], axis=1)
            for h in (2 * pair, 2 * pair + 1):
                s_ref[h] = jnp.dot(k_cat, qt_ref[slot, h], preferred_element_type=F32)

    def absorb(j, s_ref, masked):
        if masked:
            kl = lax.broadcasted_iota(jnp.int32, (tk, tq), 0)
            ql = lax.broadcasted_iota(jnp.int32, (tk, tq), 1)
            causal = kl <= ql
        mp_all = []
        for h in heads:
            s, m_old = s_ref[h], m_ref[h]
            if masked:
                s = jnp.where(causal, s, NEG)
            m_new = jnp.maximum(m_old, jnp.max(s, axis=0, keepdims=True))
            mp_all.append((m_old, m_new, jnp.exp2(s - m_new).astype(BF16)))
        for h in heads:
            m_old, m_new, p = mp_all[h]
            v1 = jnp.concatenate([v_ref[0, j, h * B_V_DIM:(h + 1) * B_V_DIM, :], ones], axis=0)
            pv = jnp.dot(v1, p, preferred_element_type=F32)
            acc_ref[h] = jnp.exp2(m_old - m_new) * acc_ref[h] + pv
            m_ref[h] = m_new

    def query_block(qi, odd, first, slot):
        cur, other = bufs[first], bufs[1 - first]
        nxt = jnp.minimum(qi + 1, n_qblk - 1)
        m_ref[...] = jnp.full(m_ref.shape, NEG, F32)
        acc_ref[...] = jnp.zeros(acc_ref.shape, F32)

        def two_blocks(i, carry):
            j = 2 * i
            scores(j + 1, other, slot)
            absorb(j, cur, False)
            scores(j + 2, cur, slot)
            absorb(j + 1, other, False)
            return carry

        lax.fori_loop(0, qi // 2, two_blocks, 0)
        build_query(nxt, 1 - slot)
        if odd:
            scores(qi, other, slot)
            absorb(qi - 1, cur, False)
            scores(0, cur, 1 - slot)
            absorb(qi, other, True)
            free = first
        else:
            scores(0, other, 1 - slot)
            absorb(qi, cur, True)
            free = 1 - first

        o_t = jnp.concatenate([acc_ref[h, :B_V_DIM] / acc_ref[h, B_V_DIM:B_V_DIM + 1] for h in heads], axis=0)
        o_ref[0, pl.ds(pl.multiple_of(qi * tq, tq), tq), :] = o_t.T.astype(o_ref.dtype)
        return free

    build_query(0, 0)
    scores(0, bufs[0], 0)

    def query_blocks(u, carry):
        first = 0
        for k in range(MLA_QUNROLL):
            first = query_block(u * MLA_QUNROLL + k, k % 2 == 1, first, k % 2)
        assert first == 0
        return carry

    lax.fori_loop(0, n_qblk // MLA_QUNROLL, query_blocks, 0)


def _mla_attn(qn_t, qr_t, kn, kr, v_t):
    bsz, n_blk, _, _ = qn_t.shape
    seq = n_blk * MLA_TQ
    n_groups = B_HEADS // MLA_GROUP
    gn, gr, gv = MLA_GROUP * B_NOPE_DIM, MLA_GROUP * B_ROPE_DIM, MLA_GROUP * B_V_DIM
    assert MLA_TQ == MLA_TK and MLA_GROUP % 2 == 0 and gn % LANES == 0 and gr % SUBLANES == 0
    assert n_blk % MLA_QUNROLL == 0
    score_buf = pltpu.VMEM((MLA_GROUP, MLA_TK, MLA_TQ), F32)
    blocked = lambda n: pl.BlockSpec((1, n_blk, n, MLA_TK), lambda b, g: (b, 0, g, 0))
    return pl.pallas_call(
        functools.partial(_mla_attn_kernel, n_qblk=n_blk),
        grid=(bsz, n_groups),
        in_specs=[blocked(gn), blocked(gr),
                  pl.BlockSpec((1, seq, gn), lambda b, g: (b, 0, g)),
                  pl.BlockSpec((1, seq, LANES), lambda b, g: (b, 0, 0)),
                  blocked(gv)],
        out_specs=pl.BlockSpec((1, seq, gv), lambda b, g: (b, 0, g)),
        out_shape=jax.ShapeDtypeStruct((bsz, seq, B_HEADS * B_V_DIM), BF16),
        scratch_shapes=[pltpu.VMEM((2, MLA_GROUP, 2 * LANES, MLA_TQ), BF16),
                        score_buf, score_buf,
                        pltpu.VMEM((MLA_GROUP, 1, MLA_TQ), F32),
                        pltpu.VMEM((MLA_GROUP, B_V_DIM + MLA_ONES, MLA_TQ), F32)],
        compiler_params=_params("parallel", "parallel"),
        name="mla_attn",
    )(qn_t, qr_t, kn, kr, v_t)


def _rope_tables(seq):
    inv_freq = ROPE_BASE ** (-jnp.arange(0, B_ROPE_DIM, 2, dtype=F32) / B_ROPE_DIM)
    ang = jnp.arange(seq, dtype=F32)[:, None] * inv_freq[None, :]
    cos, sin = jnp.cos(ang), jnp.sin(ang)
    reps = LANES // B_ROPE_DIM
    cos = jnp.tile(jnp.concatenate([cos, cos], axis=1), (1, reps))
    sin = jnp.tile(jnp.concatenate([-sin, sin], axis=1), (1, reps))
    return cos, sin, cos.T, sin.T


def kernel(x, a_w_qkv, a_w_o, kv_w_dkv, kv_norm_g, kv_w_kr, kv_w_uk, kv_w_uv, b_w_dq, b_q_norm_g, b_w_uq, b_w_o, ffn_w_in, ffn_conv_w, ffn_conv_b, ffn_w_out, ln_mix_g, ln_mix_b, ln_ffn_g, ln_ffn_b):
    bsz, seq, d = x.shape
    t = bsz * seq
    x2d = x.reshape(t, d)
    row = lambda a: a.reshape(1, -1)

    w_in_all, w_out_all = ffn_w_in.astype(BF16), ffn_w_out.astype(BF16)

    def tail(o, xin, w_o, layer):
        return _layer_tail(o.reshape(t, -1), xin, seq, layer, w_o.astype(BF16), row(ln_mix_g[layer]),
                           row(ln_mix_b[layer]), w_in_all, ffn_conv_w[layer], row(ffn_conv_b[layer]),
                           w_out_all, row(ln_ffn_g[layer]), row(ln_ffn_b[layer]))

    slopes = jnp.asarray([2.0 ** (-8.0 * (h + 1) / A_HEADS) for h in range(A_HEADS)], dtype=F32)
    qkv = _qkv_proj(x2d, a_w_qkv[0])
    o = _dilated_attn(qkv.reshape(bsz, seq, -1), slopes)
    x2d = tail(o, x2d, a_w_o[0], 0)

    cos, sin, cos_t, sin_t = _rope_tables(seq)
    w_down = jnp.concatenate([kv_w_dkv, jnp.tile(kv_w_kr, (1, LANES // B_ROPE_DIM)), b_w_dq[0]], axis=1)
    w_uq = b_w_uq[0].reshape(Q_RANK, B_HEADS, B_NOPE_DIM + B_ROPE_DIM)
    w_uqn_t = w_uq[:, :, :B_NOPE_DIM].reshape(Q_RANK, -1).T
    w_uqr_t = w_uq[:, :, B_NOPE_DIM:].reshape(Q_RANK, -1).T
    qn_t, qr_t, kn, kr, v_t = _mla_proj(x2d, seq, w_down.astype(BF16), row(kv_norm_g), row(b_q_norm_g[0]),
                                        kv_w_uk.astype(BF16), kv_w_uv.T.astype(BF16), w_uqn_t.astype(BF16),
                                        w_uqr_t.astype(BF16), cos, sin, cos_t, sin_t)
    shp = lambda a: a.reshape(bsz, seq, -1)
    o = _mla_attn(qn_t, qr_t, shp(kn), shp(kr), v_t)
    x2d = tail(o, x2d, b_w_o[0], 1)
    return x2d.reshape(bsz, seq, d)
```
